```python
import math
import jax, jax.numpy as jnp
from jax import lax
import numpy as np

D_MODEL = 1024
BATCH = 8
SEQ = 2048
DEPTH = 4
DEC_BATCH = 32
DEC_SEQ = 8
PAST_LEN = 8192
PAGE_SIZE = 128

HEAD_DIM = 64
MIX_WIDTH = D_MODEL
MEM_HEADS = 4
MEM_WIDTH = MEM_HEADS * HEAD_DIM
MEM_TOKENS = 256
TOKEN_WIDTH = MIX_WIDTH - MEM_WIDTH
SSM_GROUP_CH = 16
SSM_GROUPS = TOKEN_WIDTH // SSM_GROUP_CH
SSM_STATE = 64
SSM_MIN_STEP = 1e-3
SSM_MAX_STEP = 1e-1
SB_HEADS = TOKEN_WIDTH // HEAD_DIM
SB_BLOCK = 128
SB_BIAS_INIT = -6.0
D_FF = 4 * D_MODEL
N_MIXERS = 2
N_SSM_LAYERS = (DEPTH + 1) // 2
N_SB_LAYERS = DEPTH // 2
RMS_EPS = 1e-6
ATTN_SCALE = 1.0 / math.sqrt(HEAD_DIM)
POOL_RATIO_NUM = 5
POOL_RATIO_DEN = 4

kernel_name = "hybrid_s5_stickbreaking_memxattn_step"


def rms_norm(x, g):
    xf = x.astype(jnp.float32)
    y = xf * lax.rsqrt(jnp.mean(xf * xf, axis=-1, keepdims=True) + RMS_EPS)
    return (y * g.astype(jnp.float32)).astype(x.dtype)


def cmul(ar, ai, br, bi):
    return ar * br - ai * bi, ar * bi + ai * br


def ssm_scan(u, x0_re, x0_im, lam_re, lam_im, log_step, b_re, b_im, c_re, c_im, d_skip, w_glu):
    f32 = jnp.float32
    bsz, seq_len, _ = u.shape
    uf = u.astype(f32).reshape(bsz, seq_len, SSM_GROUPS, SSM_GROUP_CH)
    a_re = jnp.minimum(lam_re.astype(f32), -1e-4)
    a_im = lam_im.astype(f32)
    step = jnp.exp(log_step.astype(f32))[:, None]
    mag = jnp.exp(step * a_re)
    lb_re = mag * jnp.cos(step * a_im)
    lb_im = mag * jnp.sin(step * a_im)
    den = a_re * a_re + a_im * a_im
    nr, ni = cmul(lb_re - 1.0, lb_im, a_re, -a_im)
    coef_re = nr / den
    coef_im = ni / den
    bb_re, bb_im = cmul(coef_re[..., None], coef_im[..., None], b_re.astype(f32), b_im.astype(f32))
    bu_re = jnp.einsum('gpc,blgc->blgp', bb_re, uf)
    bu_im = jnp.einsum('gpc,blgc->blgp', bb_im, uf)
    a_seq_re = jnp.broadcast_to(lb_re, (1, seq_len, SSM_GROUPS, SSM_STATE))
    a_seq_im = jnp.broadcast_to(lb_im, (1, seq_len, SSM_GROUPS, SSM_STATE))

    def combine(e1, e2):
        a1r, a1i, b1r, b1i = e1
        a2r, a2i, b2r, b2i = e2
        ar, ai = cmul(a2r, a2i, a1r, a1i)
        br, bi = cmul(a2r, a2i, b1r, b1i)
        return ar, ai, br + b2r, bi + b2i

    cum_re, cum_im, st_re, st_im = lax.associative_scan(
        combine, (a_seq_re, a_seq_im, bu_re, bu_im), axis=1)
    pr, pi = cmul(cum_re, cum_im, x0_re.astype(f32)[:, None], x0_im.astype(f32)[:, None])
    st_re = st_re + pr
    st_im = st_im + pi
    y = (jnp.einsum('gcp,blgp->blgc', c_re.astype(f32), st_re)
         - jnp.einsum('gcp,blgp->blgc', c_im.astype(f32), st_im))
    y = y + d_skip.astype(f32).reshape(SSM_GROUPS, SSM_GROUP_CH) * uf
    y = y.reshape(bsz, seq_len, TOKEN_WIDTH)
    g = jax.nn.gelu(y)
    out = g * jax.nn.sigmoid(g @ w_glu.astype(f32))
    return out, st_re[:, -1], st_im[:, -1]


def stick_breaking(q, k, v, bias, q_offset):
    f32 = jnp.float32
    n_q = q.shape[1]
    b_h = bias.astype(f32)[None, :, None, None]
    outs = []
    for start in range(0, n_q, SB_BLOCK):
        end = min(start + SB_BLOCK, n_q)
        n_keys = q_offset + end
        qb = q[:, start:end]
        kb = k[:, :n_keys]
        vb = v[:, :n_keys]
        z = jnp.einsum('bqhd,bkhd->bhqk', qb, kb).astype(f32) * ATTN_SCALE + b_h
        q_pos = q_offset + start + jnp.arange(end - start, dtype=jnp.int32)
        k_pos = jnp.arange(n_keys, dtype=jnp.int32)
        mask = k_pos[None, :] < q_pos[:, None]
        log_1mb = jnp.where(mask, jax.nn.log_sigmoid(-z), 0.0)
        suffix = lax.cumsum(log_1mb, axis=3, reverse=True) - log_1mb
        w = jnp.where(mask, jnp.exp(jax.nn.log_sigmoid(z) + suffix), 0.0)
        outs.append(jnp.einsum('bhqk,bkhd->bqhd', w, vb.astype(f32)).astype(q.dtype))
    return jnp.concatenate(outs, axis=1)


def mem_kv(mem, norm_mem, w_mem_kv):
    bsz, n_mem, _ = mem.shape
    kv = rms_norm(mem, norm_mem) @ w_mem_kv
    mk, mv = jnp.split(kv, [MEM_WIDTH], axis=-1)
    return (mk.reshape(bsz, n_mem, MEM_HEADS, HEAD_DIM),
            mv.reshape(bsz, n_mem, MEM_HEADS, HEAD_DIM))


def mem_attend(qm, mk, mv):
    bsz, seq_len, _ = qm.shape
    q = qm.reshape(bsz, seq_len, MEM_HEADS, HEAD_DIM)
    s = jnp.einsum('bqhd,bmhd->bhqm', q, mk.astype(q.dtype)).astype(jnp.float32) * ATTN_SCALE
    p = jax.nn.softmax(s, axis=-1)
    o = jnp.einsum('bhqm,bmhd->bqhd', p, mv.astype(jnp.float32))
    return o.reshape(bsz, seq_len, MEM_WIDTH).astype(qm.dtype)


def ssm_sublayer(h, w_in, x0_re, x0_im, lam_re, lam_im, log_step, b_re, b_im, c_re, c_im, d_skip, w_glu):
    proj = h @ w_in
    u, qm = jnp.split(proj, [TOKEN_WIDTH], axis=-1)
    out, fin_re, fin_im = ssm_scan(u, x0_re, x0_im, lam_re, lam_im, log_step,
                                   b_re, b_im, c_re, c_im, d_skip, w_glu)
    return out.astype(h.dtype), qm, fin_re, fin_im


def sb_sublayer(h, w_in, bias, k_past, v_past):
    bsz, seq_len, _ = h.shape
    proj = h @ w_in
    q, k, v, qm = jnp.split(proj, [TOKEN_WIDTH, 2 * TOKEN_WIDTH, 3 * TOKEN_WIDTH], axis=-1)
    q = q.reshape(bsz, seq_len, SB_HEADS, HEAD_DIM)
    k = k.reshape(bsz, seq_len, SB_HEADS, HEAD_DIM)
    v = v.reshape(bsz, seq_len, SB_HEADS, HEAD_DIM)
    k_all = jnp.concatenate([k_past.astype(k.dtype), k], axis=1)
    v_all = jnp.concatenate([v_past.astype(v.dtype), v], axis=1)
    o = stick_breaking(q, k_all, v_all, bias, k_past.shape[1])
    return o.reshape(bsz, seq_len, TOKEN_WIDTH), qm, k, v


def mlp(h, w_up, w_down):
    return jnp.square(jax.nn.relu(h @ w_up)) @ w_down


def _normal(key, shape, scale):
    return jax.random.normal(key, shape, jnp.float32) * scale


def setup_inputs(seed: int = 0) -> dict:
    key = jax.random.key(seed)
    ks = jax.random.split(key, 32)
    n_pages = PAST_LEN // PAGE_SIZE
    n_used = DEC_BATCH * n_pages
    n_pool = (n_used * POOL_RATIO_NUM) // POOL_RATIO_DEN
    page_table = jax.random.permutation(ks[0], n_pool)[:n_used].reshape(DEC_BATCH, n_pages).astype(jnp.int32)
    lam_im0 = jnp.pi * jnp.arange(SSM_STATE, dtype=jnp.float32)
    return {
        "x_prompt": _normal(ks[1], (BATCH, SEQ, D_MODEL), 1.0),
        "x_sample": _normal(ks[2], (DEC_BATCH, DEC_SEQ, D_MODEL), 1.0),
        "mem_prompt": _normal(ks[3], (BATCH, MEM_TOKENS, D_MODEL), 1.0),
        "cache_sb_k": _normal(ks[4], (N_SB_LAYERS, n_pool, PAGE_SIZE, SB_HEADS, HEAD_DIM), 1.0),
        "cache_sb_v": _normal(ks[5], (N_SB_LAYERS, n_pool, PAGE_SIZE, SB_HEADS, HEAD_DIM), 1.0),
        "page_table": page_table,
        "cache_mem_k": _normal(ks[6], (DEPTH, DEC_BATCH, MEM_TOKENS, MEM_HEADS, HEAD_DIM), 1.0),
        "cache_mem_v": _normal(ks[7], (DEPTH, DEC_BATCH, MEM_TOKENS, MEM_HEADS, HEAD_DIM), 1.0),
        "state_ssm_re": _normal(ks[8], (N_SSM_LAYERS, DEC_BATCH, SSM_GROUPS, SSM_STATE), 0.1),
        "state_ssm_im": _normal(ks[9], (N_SSM_LAYERS, DEC_BATCH, SSM_GROUPS, SSM_STATE), 0.1),
        "norm_mix": 1.0 + _normal(ks[10], (DEPTH, D_MODEL), 0.01),
        "norm_mlp": 1.0 + _normal(ks[11], (DEPTH, D_MODEL), 0.01),
        "norm_mem": 1.0 + _normal(ks[12], (D_MODEL,), 0.01),
        "norm_final": 1.0 + _normal(ks[13], (D_MODEL,), 0.01),
        "w_in_ssm": _normal(ks[14], (N_SSM_LAYERS, D_MODEL, TOKEN_WIDTH + MEM_WIDTH), D_MODEL ** -0.5),
        "ssm_lambda_re": -0.5 + _normal(ks[15], (N_SSM_LAYERS, SSM_GROUPS, SSM_STATE), 0.01),
        "ssm_lambda_im": lam_im0 + _normal(ks[16], (N_SSM_LAYERS, SSM_GROUPS, SSM_STATE), 0.01),
        "ssm_log_step": jax.random.uniform(ks[17], (N_SSM_LAYERS, SSM_GROUPS), jnp.float32,
                                           math.log(SSM_MIN_STEP), math.log(SSM_MAX_STEP)),
        "ssm_b_re": _normal(ks[18], (N_SSM_LAYERS, SSM_GROUPS, SSM_STATE, SSM_GROUP_CH), (2 * SSM_GROUP_CH) ** -0.5),
        "ssm_b_im": _normal(ks[19], (N_SSM_LAYERS, SSM_GROUPS, SSM_STATE, SSM_GROUP_CH), (2 * SSM_GROUP_CH) ** -0.5),
        "ssm_c_re": _normal(ks[20], (N_SSM_LAYERS, SSM_GROUPS, SSM_GROUP_CH, SSM_STATE), SSM_STATE ** -0.5),
        "ssm_c_im": _normal(ks[21], (N_SSM_LAYERS, SSM_GROUPS, SSM_GROUP_CH, SSM_STATE), SSM_STATE ** -0.5),
        "ssm_d": _normal(ks[22], (N_SSM_LAYERS, TOKEN_WIDTH), 1.0),
        "ssm_w_glu": _normal(ks[23], (N_SSM_LAYERS, TOKEN_WIDTH, TOKEN_WIDTH), TOKEN_WIDTH ** -0.5),
        "w_in_sb": _normal(ks[24], (N_SB_LAYERS, D_MODEL, 3 * TOKEN_WIDTH + MEM_WIDTH), D_MODEL ** -0.5),
        "sb_bias": SB_BIAS_INIT + _normal(ks[29], (N_SB_LAYERS, SB_HEADS), 0.5),
        "w_mem_kv": _normal(ks[25], (DEPTH, D_MODEL, 2 * MEM_WIDTH), D_MODEL ** -0.5),
        "w_out": _normal(ks[26], (DEPTH, MIX_WIDTH, D_MODEL), MIX_WIDTH ** -0.5),
        "w_up": _normal(ks[27], (DEPTH, D_MODEL, D_FF), D_MODEL ** -0.5),
        "w_down": _normal(ks[28], (DEPTH, D_FF, D_MODEL), D_FF ** -0.5),
    }


def reference(x_prompt, x_sample, mem_prompt, cache_sb_k, cache_sb_v, page_table,
              cache_mem_k, cache_mem_v, state_ssm_re, state_ssm_im,
              norm_mix, norm_mlp, norm_mem, norm_final,
              w_in_ssm, ssm_lambda_re, ssm_lambda_im, ssm_log_step,
              ssm_b_re, ssm_b_im, ssm_c_re, ssm_c_im, ssm_d, ssm_w_glu,
              w_in_sb, sb_bias, w_mem_kv, w_out, w_up, w_down):
    bsz_p = x_prompt.shape[0]
    bsz_s = x_sample.shape[0]
    past_len = page_table.shape[1] * PAGE_SIZE
    xp = x_prompt
    xs = x_sample
    ssm_re_p, ssm_im_p, ssm_re_s, ssm_im_s = [], [], [], []
    sb_k_p, sb_v_p, sb_k_s, sb_v_s = [], [], [], []
    mem_k_p, mem_v_p = [], []
    for i in range(DEPTH):
        j = i // N_MIXERS
        mk_p, mv_p = mem_kv(mem_prompt, norm_mem, w_mem_kv[i])
        mem_k_p.append(mk_p)
        mem_v_p.append(mv_p)
        mk_s = cache_mem_k[i]
        mv_s = cache_mem_v[i]
        hp = rms_norm(xp, norm_mix[i])
        hs = rms_norm(xs, norm_mix[i])
        if i % N_MIXERS == 0:
            ssm_params = (ssm_lambda_re[j], ssm_lambda_im[j], ssm_log_step[j], ssm_b_re[j], ssm_b_im[j],
                          ssm_c_re[j], ssm_c_im[j], ssm_d[j], ssm_w_glu[j])
            zero_state = jnp.zeros((bsz_p, SSM_GROUPS, SSM_STATE), jnp.float32)
            mix_p, qm_p, fr_p, fi_p = ssm_sublayer(hp, w_in_ssm[j], zero_state, zero_state, *ssm_params)
            mix_s, qm_s, fr_s, fi_s = ssm_sublayer(hs, w_in_ssm[j], state_ssm_re[j], state_ssm_im[j], *ssm_params)
            ssm_re_p.append(fr_p)
            ssm_im_p.append(fi_p)
            ssm_re_s.append(fr_s)
            ssm_im_s.append(fi_s)
        else:
            empty = jnp.zeros((bsz_p, 0, SB_HEADS, HEAD_DIM), xp.dtype)
            mix_p, qm_p, kp, vp = sb_sublayer(hp, w_in_sb[j], sb_bias[j], empty, empty)
            k_past = cache_sb_k[j][page_table].reshape(bsz_s, past_len, SB_HEADS, HEAD_DIM)
            v_past = cache_sb_v[j][page_table].reshape(bsz_s, past_len, SB_HEADS, HEAD_DIM)
            mix_s, qm_s, ks_new, vs_new = sb_sublayer(hs, w_in_sb[j], sb_bias[j], k_past, v_past)
            sb_k_p.append(kp)
            sb_v_p.append(vp)
            sb_k_s.append(ks_new)
            sb_v_s.append(vs_new)
        xp = xp + (jnp.concatenate([mix_p, mem_attend(qm_p, mk_p, mv_p)], axis=-1) @ w_out[i]).astype(xp.dtype)
        xs = xs + (jnp.concatenate([mix_s, mem_attend(qm_s, mk_s, mv_s)], axis=-1) @ w_out[i]).astype(xs.dtype)
        xp = xp + mlp(rms_norm(xp, norm_mlp[i]), w_up[i], w_down[i]).astype(xp.dtype)
        xs = xs + mlp(rms_norm(xs, norm_mlp[i]), w_up[i], w_down[i]).astype(xs.dtype)
    y_prompt = rms_norm(xp, norm_final)
    y_sample = rms_norm(xs, norm_final)
    return (y_prompt, y_sample,
            jnp.stack(ssm_re_p), jnp.stack(ssm_im_p), jnp.stack(ssm_re_s), jnp.stack(ssm_im_s),
            jnp.stack(sb_k_p), jnp.stack(sb_v_p), jnp.stack(sb_k_s), jnp.stack(sb_v_s),
            jnp.stack(mem_k_p), jnp.stack(mem_v_p))
```

```python
import functools
import math

import jax
import jax.numpy as jnp
from jax import lax
from jax.experimental import pallas as pl
from jax.experimental.pallas import tpu as pltpu

F32 = jnp.float32
BF16 = jnp.bfloat16

RMS_EPS = 1e-6
HEAD_DIM = 64
ATTN_SCALE = 1.0 / math.sqrt(HEAD_DIM)
MEM_HEADS = 4
MEM_WIDTH = MEM_HEADS * HEAD_DIM
SSM_GROUP_CH = 16
SSM_STATE = 64
PAGE_SIZE = 128

LANES = 128
SLAB_GROUPS = LANES // SSM_GROUP_CH
SLAB_STATE = SLAB_GROUPS * SSM_STATE
VMEM_LIMIT = 56 * 1024 * 1024


def _cparams(sem):
    return pltpu.CompilerParams(dimension_semantics=sem, vmem_limit_bytes=VMEM_LIMIT)


def _rmsnorm(x, g):
    ms = jnp.mean(x * x, axis=-1, keepdims=True)
    return x * lax.rsqrt(ms + RMS_EPS) * g


def _dot(a, b):
    return jnp.dot(a, b, preferred_element_type=F32)


def _dot_nt(a, b):
    return lax.dot_general(a, b, (((1,), (1,)), ((), ())), preferred_element_type=F32)


def _norm_proj_kernel(x_ref, g_ref, w_ref, *o_refs, widths):
    h = _rmsnorm(x_ref[...], g_ref[...]).astype(BF16)
    r = _dot(h, w_ref[...])
    off = 0
    for o_ref, wd in zip(o_refs, widths):
        o_ref[...] = r[:, off:off + wd]
        off += wd


def norm_proj(x, g, w_bf16, widths, tm):
    t, d = x.shape
    n = w_bf16.shape[1]
    assert sum(widths) == n and t % tm == 0
    return pl.pallas_call(
        functools.partial(_norm_proj_kernel, widths=widths),
        grid=(t // tm,),
        in_specs=[
            pl.BlockSpec((tm, d), lambda i: (i, 0)),
            pl.BlockSpec((1, d), lambda i: (0, 0)),
            pl.BlockSpec((d, n), lambda i: (0, 0)),
        ],
        out_specs=[pl.BlockSpec((tm, wd), lambda i: (i, 0)) for wd in widths],
        out_shape=[jax.ShapeDtypeStruct((t, wd), F32) for wd in widths],
        compiler_params=_cparams(("parallel",)),
        name="norm_proj",
    )(x, g.reshape(1, d), w_bf16)


def _mem_kv_kernel(x_ref, g_ref, w_ref, k_ref, v_ref, h_scr):
    @pl.when(pl.program_id(0) == 0)
    def _():
        h_scr[...] = _rmsnorm(x_ref[...], g_ref[...]).astype(BF16)

    r = _dot(h_scr[...], w_ref[...])
    k_ref[...] = r[:, :MEM_WIDTH]
    v_ref[...] = r[:, MEM_WIDTH:]


def mem_kv_all(mem, g, w_bf16):
    t, d = mem.shape
    depth = w_bf16.shape[0]
    return pl.pallas_call(
        _mem_kv_kernel,
        grid=(depth,),
        in_specs=[
            pl.BlockSpec((t, d), lambda l: (0, 0)),
            pl.BlockSpec((1, d), lambda l: (0, 0)),
            pl.BlockSpec((None, d, 2 * MEM_WIDTH), lambda l: (l, 0, 0)),
        ],
        out_specs=[pl.BlockSpec((None, t, MEM_WIDTH), lambda l: (l, 0, 0))] * 2,
        out_shape=[jax.ShapeDtypeStruct((depth, t, MEM_WIDTH), F32)] * 2,
        scratch_shapes=[pltpu.VMEM((t, d), BF16)],
        compiler_params=_cparams(("arbitrary",)),
        name="mem_kv",
    )(mem, g.reshape(1, d), w_bf16)


def _mlp_kernel(x_ref, g_ref, wu_ref, wd_ref, gf_ref, *refs, nf, final_norm):
    if final_norm:
        o_ref, y_ref, h_scr, acc_scr = refs
    else:
        o_ref, h_scr, acc_scr = refs
    f = pl.program_id(1)

    @pl.when(f == 0)
    def _():
        h_scr[...] = _rmsnorm(x_ref[...], g_ref[...]).astype(BF16)
        acc_scr[...] = jnp.zeros_like(acc_scr)

    a = _dot(h_scr[...], wu_ref[...])
    a = jnp.square(jnp.maximum(a, 0.0)).astype(BF16)
    acc_scr[...] += _dot(a, wd_ref[...])

    @pl.when(f == nf - 1)
    def _():
        xn = x_ref[...] + acc_scr[...]
        o_ref[...] = xn
        if final_norm:
            y_ref[...] = _rmsnorm(xn, gf_ref[...])


def mlp(x, g, wu_bf16, wd_bf16, g_final, tm, tf, final_norm):
    t, d = x.shape
    ff = wu_bf16.shape[1]
    nf = ff // tf
    n_out = 2 if final_norm else 1
    outs = pl.pallas_call(
        functools.partial(_mlp_kernel, nf=nf, final_norm=final_norm),
        grid=(t // tm, nf),
        in_specs=[
            pl.BlockSpec((tm, d), lambda i, f: (i, 0)),
            pl.BlockSpec((1, d), lambda i, f: (0, 0)),
            pl.BlockSpec((d, tf), lambda i, f: (0, f)),
            pl.BlockSpec((tf, d), lambda i, f: (f, 0)),
            pl.BlockSpec((1, d), lambda i, f: (0, 0)),
        ],
        out_specs=[pl.BlockSpec((tm, d), lambda i, f: (i, 0))] * n_out,
        out_shape=[jax.ShapeDtypeStruct((t, d), F32)] * n_out,
        scratch_shapes=[pltpu.VMEM((tm, d), BF16), pltpu.VMEM((tm, d), F32)],
        compiler_params=_cparams(("parallel", "arbitrary")),
        name="mlp",
    )(x, g.reshape(1, d), wu_bf16, wd_bf16, g_final.reshape(1, d))
    return outs if final_norm else (outs[0], None)


def _glu_kernel(y_ref, w_ref, o_ref):
    g = jax.nn.gelu(y_ref[...])
    gm = _dot(g.astype(BF16), w_ref[...])
    o_ref[...] = g * jax.nn.sigmoid(gm)


def glu(y, w_bf16, tm):
    t, n = y.shape
    return pl.pallas_call(
        _glu_kernel,
        grid=(t // tm,),
        in_specs=[pl.BlockSpec((tm, n), lambda i: (i, 0)), pl.BlockSpec((n, n), lambda i: (0, 0))],
        out_specs=pl.BlockSpec((tm, n), lambda i: (i, 0)),
        out_shape=jax.ShapeDtypeStruct((t, n), F32),
        compiler_params=_cparams(("parallel",)),
        name="glu",
    )(y, w_bf16)


def _mem_attn_kernel(q_ref, k_ref, v_ref, o_ref):
    q = q_ref[...] * ATTN_SCALE
    kb = k_ref[...].astype(BF16)
    vb = v_ref[...].astype(BF16)
    lane_head = lax.broadcasted_iota(jnp.int32, (1, MEM_WIDTH), 1) // HEAD_DIM
    out = jnp.zeros(q.shape, F32)
    for h in range(MEM_HEADS):
        m = lane_head == h
        s = _dot_nt(jnp.where(m, q, 0.0).astype(BF16), kb)
        p = jnp.exp(s - jnp.max(s, axis=-1, keepdims=True))
        p = p / jnp.sum(p, axis=-1, keepdims=True)
        out = out + jnp.where(m, _dot(p.astype(BF16), vb), 0.0)
    o_ref[...] = out


def mem_attn(qm, mk, mv, n_seq, tm):
    t = qm.shape[0]
    seq_len = t // n_seq
    n_mem = mk.shape[0] // n_seq
    nt = seq_len // tm
    return pl.pallas_call(
        _mem_attn_kernel,
        grid=(n_seq, nt),
        in_specs=[
            pl.BlockSpec((tm, MEM_WIDTH), lambda b, i: (b * nt + i, 0)),
            pl.BlockSpec((n_mem, MEM_WIDTH), lambda b, i: (b, 0)),
            pl.BlockSpec((n_mem, MEM_WIDTH), lambda b, i: (b, 0)),
        ],
        out_specs=pl.BlockSpec((tm, MEM_WIDTH), lambda b, i: (b * nt + i, 0)),
        out_shape=jax.ShapeDtypeStruct((t, MEM_WIDTH), F32),
        compiler_params=_cparams(("parallel", "parallel")),
        name="mem_attn",
    )(qm, mk, mv)


def _out_proj_kernel(x_ref, mix_ref, mem_ref, w1_ref, w2_ref, o_ref):
    acc = _dot(mix_ref[...].astype(BF16), w1_ref[...])
    acc = acc + _dot(mem_ref[...].astype(BF16), w2_ref[...])
    o_ref[...] = x_ref[...] + acc


def out_proj(x, mix, mem_o, w_bf16, tm):
    t, d = x.shape
    n1 = mix.shape[1]
    n2 = mem_o.shape[1]
    return pl.pallas_call(
        _out_proj_kernel,
        grid=(t // tm,),
        in_specs=[
            pl.BlockSpec((tm, d), lambda i: (i, 0)),
            pl.BlockSpec((tm, n1), lambda i: (i, 0)),
            pl.BlockSpec((tm, n2), lambda i: (i, 0)),
            pl.BlockSpec((n1, d), lambda i: (0, 0)),
            pl.BlockSpec((n2, d), lambda i: (0, 0)),
        ],
        out_specs=pl.BlockSpec((tm, d), lambda i: (i, 0)),
        out_shape=jax.ShapeDtypeStruct((t, d), F32),
        compiler_params=_cparams(("parallel",)),
        name="out_proj",
    )(x, mix, mem_o, w_bf16[:n1], w_bf16[n1:])


def _cmul(ar, ai, br, bi):
    return ar * br - ai * bi, ar * bi + ai * br


def _ssm_kernel(u_ref, x0r_ref, x0i_ref, m_ref, bc_ref, cc_ref, d_ref, apr_ref, api_ref,
                y_ref, xr_ref, xi_ref, *, lc, nc):
    rows = u_ref.shape[0]
    us = [u_ref[:, s, :] for s in range(lc)]
    ucat = jnp.concatenate([u.astype(BF16) for u in us], axis=1)
    y = _dot(ucat, m_ref[...])
    v = _dot(ucat, bc_ref[...])
    vr, vi = v[:, :SLAB_STATE], v[:, SLAB_STATE:]
    x0r, x0i = x0r_ref[...], x0i_ref[...]
    ar, ai = apr_ref[0:1, :], api_ref[0:1, :]
    inj_r, inj_i = _cmul(ar, ai, x0r, x0i)
    if nc == 1:
        xr, xi = vr + inj_r, vi + inj_i
        pr, pi = x0r, x0i
        xr_ref[...] = xr
        xi_ref[...] = xi
    else:
        row = lax.broadcasted_iota(jnp.int32, (rows, 1), 0)
        vr = vr + jnp.where(row == 0, inj_r, 0.0)
        vi = vi + jnp.where(row == 0, inj_i, 0.0)
        for k in range(nc.bit_length() - 1):
            sh = 1 << k
            sr = jnp.where(row >= sh, pltpu.roll(vr, sh, 0), 0.0)
            si = jnp.where(row >= sh, pltpu.roll(vi, sh, 0), 0.0)
            dr, di = _cmul(apr_ref[k:k + 1, :], api_ref[k:k + 1, :], sr, si)
            vr, vi = vr + dr, vi + di
        xr_ref[...] = vr[rows - 1:rows, :]
        xi_ref[...] = vi[rows - 1:rows, :]
        pr = jnp.where(row >= 1, pltpu.roll(vr, 1, 0), x0r)
        pi = jnp.where(row >= 1, pltpu.roll(vi, 1, 0), x0i)
    prev = jnp.concatenate([pr.astype(BF16), pi.astype(BF16)], axis=1)
    y = y + _dot(prev, cc_ref[...])
    d = d_ref[...]
    for s in range(lc):
        y_ref[:, s, :] = y[:, s * LANES:(s + 1) * LANES] + d * us[s]


def ssm_apply(u, x0r, x0i, mats, lc, nc, nb):
    m, bc, cc, dsk, apr, api = mats
    t, width = u.shape
    n_slab = width // LANES
    rows_total = t // lc
    rows = nb * nc
    n_blk = rows_total // rows
    u3 = u.reshape(rows_total, lc, width)
    kdim = lc * LANES
    n_pow = apr.shape[1]
    y3, xr, xi = pl.pallas_call(
        functools.partial(_ssm_kernel, lc=lc, nc=nc),
        grid=(n_slab, n_blk),
        in_specs=[
            pl.BlockSpec((rows, lc, LANES), lambda s, b: (b, 0, s)),
            pl.BlockSpec((None, None, nb, SLAB_STATE), lambda s, b: (s, b, 0, 0)),
            pl.BlockSpec((None, None, nb, SLAB_STATE), lambda s, b: (s, b, 0, 0)),
            pl.BlockSpec((None, kdim, kdim), lambda s, b: (s, 0, 0)),
            pl.BlockSpec((None, kdim, 2 * SLAB_STATE), lambda s, b: (s, 0, 0)),
            pl.BlockSpec((None, 2 * SLAB_STATE, kdim), lambda s, b: (s, 0, 0)),
            pl.BlockSpec((None, 1, LANES), lambda s, b: (s, 0, 0)),
            pl.BlockSpec((None, n_pow, SLAB_STATE), lambda s, b: (s, 0, 0)),
            pl.BlockSpec((None, n_pow, SLAB_STATE), lambda s, b: (s, 0, 0)),
        ],
        out_specs=[
            pl.BlockSpec((rows, lc, LANES), lambda s, b: (b, 0, s)),
            pl.BlockSpec((None, None, nb, SLAB_STATE), lambda s, b: (s, b, 0, 0)),
            pl.BlockSpec((None, None, nb, SLAB_STATE), lambda s, b: (s, b, 0, 0)),
        ],
        out_shape=[
            jax.ShapeDtypeStruct((rows_total, lc, width), F32),
            jax.ShapeDtypeStruct((n_slab, n_blk, nb, SLAB_STATE), F32),
            jax.ShapeDtypeStruct((n_slab, n_blk, nb, SLAB_STATE), F32),
        ],
        compiler_params=_cparams(("arbitrary", "arbitrary")),
        name="ssm",
    )(u3, x0r, x0i, m, bc, cc, dsk, apr, api)
    return y3.reshape(t, width), xr, xi


def ssm_matrices(lam_re, lam_im, log_step, b_re, b_im, c_re, c_im, d_skip, lc, nc):
    hp = lax.Precision.HIGHEST
    n_g, n_p, n_c = b_re.shape
    n_slab = n_g // SLAB_GROUPS
    a_re = jnp.minimum(lam_re.astype(F32), -1e-4)
    a_im = lam_im.astype(F32)
    step = jnp.exp(log_step.astype(F32))[:, None]
    mag = jnp.exp(step * a_re)
    lb_re = mag * jnp.cos(step * a_im)
    lb_im = mag * jnp.sin(step * a_im)
    den = a_re * a_re + a_im * a_im
    nr, ni = _cmul(lb_re - 1.0, lb_im, a_re, -a_im)
    coef_re, coef_im = nr / den, ni / den
    bb_re, bb_im = _cmul(coef_re[..., None], coef_im[..., None], b_re.astype(F32), b_im.astype(F32))
    pw_re, pw_im = [jnp.ones_like(lb_re)], [jnp.zeros_like(lb_im)]
    for _ in range(lc):
        r, i = _cmul(pw_re[-1], pw_im[-1], lb_re, lb_im)
        pw_re.append(r)
        pw_im.append(i)
    pw_re, pw_im = jnp.stack(pw_re), jnp.stack(pw_im)
    e_re, e_im = _cmul(pw_re[:lc, :, :, None], pw_im[:lc, :, :, None], bb_re[None], bb_im[None])
    cr, ci = c_re.astype(F32), c_im.astype(F32)
    kern = (jnp.einsum('gcp,jgpd->jgdc', cr, e_re, precision=hp)
            - jnp.einsum('gcp,jgpd->jgdc', ci, e_im, precision=hp))
    eye = jnp.eye(SLAB_GROUPS, dtype=F32)
    s_idx = jnp.arange(lc)
    lag = s_idx[None, :] - s_idx[:, None]
    kst = jnp.where((lag >= 0)[:, :, None, None, None], kern[jnp.clip(lag, 0, lc - 1)], 0.0)
    kst = kst.reshape(lc, lc, n_slab, SLAB_GROUPS, n_c, n_c)
    kst = jnp.transpose(kst, (2, 0, 3, 4, 1, 5))
    m = kst[:, :, :, :, :, None, :] * eye[None, None, :, None, None, :, None]
    m = m.reshape(n_slab, lc * LANES, lc * LANES).astype(BF16)
    ev = jnp.stack([e_re[::-1], e_im[::-1]])
    ev = ev.reshape(2, lc, n_slab, SLAB_GROUPS, n_p, n_c)
    ev = jnp.transpose(ev, (2, 1, 3, 5, 0, 4))
    bc = ev[:, :, :, :, :, None, :] * eye[None, None, :, None, None, :, None]
    bc = bc.reshape(n_slab, lc * LANES, 2 * SLAB_STATE).astype(BF16)
    q_re, q_im = _cmul(cr[None], ci[None], pw_re[1:, :, None, :], pw_im[1:, :, None, :])
    qv = jnp.stack([q_re, -q_im]).reshape(2, lc, n_slab, SLAB_GROUPS, n_c, n_p)
    qv = jnp.transpose(qv, (2, 0, 3, 5, 1, 4))
    cc = qv[:, :, :, :, :, None, :] * eye[None, None, :, None, None, :, None]
    cc = cc.reshape(n_slab, 2 * SLAB_STATE, lc * LANES).astype(BF16)
    apr, api = [pw_re[lc]], [pw_im[lc]]
    for _ in range(max(nc.bit_length() - 2, 0)):
        r, i = _cmul(apr[-1], api[-1], apr[-1], api[-1])
        apr.append(r)
        api.append(i)
    apr = jnp.transpose(jnp.stack(apr).reshape(-1, n_slab, SLAB_STATE), (1, 0, 2))
    api = jnp.transpose(jnp.stack(api).reshape(-1, n_slab, SLAB_STATE), (1, 0, 2))
    dsk = d_skip.astype(F32).reshape(n_slab, 1, LANES)
    return m, bc, cc, dsk, apr, api


def _state_to_slabs(x, nb):
    n_seq, n_g, n_p = x.shape
    n_slab = n_g // SLAB_GROUPS
    x = x.reshape(n_seq, n_slab, SLAB_STATE)
    return jnp.transpose(x, (1, 0, 2)).reshape(n_slab, n_seq // nb, nb, SLAB_STATE)


def _slabs_to_state(x, n_p):
    n_slab, n_blk, nb, _ = x.shape
    x = jnp.transpose(x.reshape(n_slab, n_blk * nb, SLAB_STATE), (1, 0, 2))
    return x.reshape(n_blk * nb, n_slab * SLAB_GROUPS, n_p)


def _sb_block(qh, kblk, vblk, bias, tri, carry, mask):
    z = _dot_nt(qh, kblk) + bias
    lp = jnp.log(1.0 + jnp.exp(-jnp.abs(z)))
    l1m = -jnp.maximum(z, 0.0) - lp
    if mask is not None:
        l1m = jnp.where(mask, l1m, 0.0)
    suffix = _dot(l1m.astype(BF16), tri) + carry
    w = jnp.exp(z + l1m + suffix)
    if mask is not None:
        w = jnp.where(mask, w, 0.0)
    o = _dot(w.astype(BF16), vblk)
    return o, carry + jnp.sum(l1m, axis=-1, keepdims=True)


def _sb_prompt_kernel(bias_ref, q_ref, k_ref, v_ref, tri_ref, o_ref, *, tq):
    pair = pl.program_id(1)
    i = pl.program_id(2)
    q = q_ref[...] * ATTN_SCALE
    tri = tri_ref[...]
    lane_head = lax.broadcasted_iota(jnp.int32, (1, LANES), 1) // HEAD_DIM
    qpos = lax.broadcasted_iota(jnp.int32, (tq, tq), 0)
    kpos = lax.broadcasted_iota(jnp.int32, (tq, tq), 1)
    diag_mask = kpos < qpos
    out = jnp.zeros((tq, LANES), F32)
    for hh in range(LANES // HEAD_DIM):
        m = lane_head == hh
        qh = jnp.where(m, q, 0.0).astype(BF16)
        bias = bias_ref[pair * (LANES // HEAD_DIM) + hh]
        start = pl.multiple_of(i * tq, tq)
        o, carry = _sb_block(qh, k_ref[pl.ds(start, tq), :].astype(BF16),
                             v_ref[pl.ds(start, tq), :].astype(BF16), bias, tri,
                             jnp.zeros((tq, 1), F32), diag_mask)

        def body(it, oc):
            o_acc, c = oc
            st = pl.multiple_of((i - 1 - it) * tq, tq)
            ob, c = _sb_block(qh, k_ref[pl.ds(st, tq), :].astype(BF16),
                              v_ref[pl.ds(st, tq), :].astype(BF16), bias, tri, c, None)
            return o_acc + ob, c

        o, _ = lax.fori_loop(0, i, body, (o, carry))
        out = jnp.where(m, o, out)
    o_ref[...] = out


def _tri(n):
    r = jnp.arange(n)
    return (r[:, None] > r[None, :]).astype(BF16)


def sb_prompt(q, k, v, bias, n_seq, tq):
    t, width = q.shape
    seq_len = t // n_seq
    nq = seq_len // tq
    n_pair = width // LANES
    return pl.pallas_call(
        functools.partial(_sb_prompt_kernel, tq=tq),
        grid=(n_seq, n_pair, nq),
        in_specs=[
            pl.BlockSpec(memory_space=pltpu.SMEM),
            pl.BlockSpec((tq, LANES), lambda b, p, i: (b * nq + i, p)),
            pl.BlockSpec((seq_len, LANES), lambda b, p, i: (b, p)),
            pl.BlockSpec((seq_len, LANES), lambda b, p, i: (b, p)),
            pl.BlockSpec((tq, tq), lambda b, p, i: (0, 0)),
        ],
        out_specs=pl.BlockSpec((tq, LANES), lambda b, p, i: (b * nq + i, p)),
        out_shape=jax.ShapeDtypeStruct((t, width), F32),
        compiler_params=_cparams(("parallel", "parallel", "arbitrary")),
        name="sb_prompt",
    )(bias.astype(F32), q, k, v, _tri(tq))


def _sb_sample_kernel(pt_ref, q_ref, kn_ref, vn_ref, bias_ref, tri_ref, *refs, n_q, n_pg, kb):
    k_refs = refs[:n_pg]
    v_refs = refs[n_pg:2 * n_pg]
    o_ref = refs[2 * n_pg]
    acc_scr, carry_scr = refs[2 * n_pg + 1:]
    i = pl.program_id(1)
    width = q_ref.shape[1]
    n_heads = width // HEAD_DIM
    rows = n_heads * n_q
    row = lax.broadcasted_iota(jnp.int32, (rows, 1), 0)
    head_mask = (lax.broadcasted_iota(jnp.int32, (1, width), 1) // HEAD_DIM) == (row // n_q)
    q = q_ref[...] * ATTN_SCALE
    qbd = jnp.where(head_mask, jnp.concatenate([q] * n_heads, axis=0), 0.0).astype(BF16)
    bias = bias_ref[...]
    tri = tri_ref[...]

    @pl.when(i == 0)
    def _():
        kpos = lax.broadcasted_iota(jnp.int32, (rows, kb), 1)
        mask = kpos < (row % n_q)
        o, c = _sb_block(qbd, kn_ref[...].astype(BF16), vn_ref[...].astype(BF16), bias, tri,
                         jnp.zeros((rows, 1), F32), mask)
        acc_scr[...] = o
        carry_scr[...] = c

    acc = acc_scr[...]
    carry = carry_scr[...]
    per_blk = kb // PAGE_SIZE
    for j in range(n_pg // per_blk):
        kblk = jnp.concatenate([k_refs[j * per_blk + r][...] for r in reversed(range(per_blk))], axis=0)
        vblk = jnp.concatenate([v_refs[j * per_blk + r][...] for r in reversed(range(per_blk))], axis=0)
        o, carry = _sb_block(qbd, kblk.astype(BF16), vblk.astype(BF16), bias, tri, carry, None)
        acc = acc + o
    acc_scr[...] = acc
    carry_scr[...] = carry

    @pl.when(i == pl.num_programs(1) - 1)
    def _():
        a = jnp.where(head_mask, acc, 0.0)
        out = a[0:n_q]
        for h in range(1, n_heads):
            out = out + a[h * n_q:(h + 1) * n_q]
        o_ref[...] = out


def sb_sample(q, k_new, v_new, cache_k, cache_v, layer, page_table, bias, n_q, n_pg=8, kb=256):
    t, width = q.shape
    n_seq = t // n_q
    n_pages = page_table.shape[1]
    n_heads = width // HEAD_DIM
    rows = n_heads * n_q
    pad = jnp.zeros((n_seq, kb - n_q, width), F32)
    kn = jnp.concatenate([k_new.reshape(n_seq, n_q, width), pad], axis=1)
    vn = jnp.concatenate([v_new.reshape(n_seq, n_q, width), pad], axis=1)
    bias_col = jnp.repeat(bias.astype(F32), n_q).reshape(rows, 1)

    def page_spec(r):
        return pl.BlockSpec((None, None, PAGE_SIZE, width),
                            lambda b, i, pt: (layer, pt[b, n_pages - 1 - (i * n_pg + r)], 0, 0))

    grid_spec = pltpu.PrefetchScalarGridSpec(
        num_scalar_prefetch=1,
        grid=(n_seq, n_pages // n_pg),
        in_specs=[
            pl.BlockSpec((n_q, width), lambda b, i, pt: (b, 0)),
            pl.BlockSpec((None, kb, width), lambda b, i, pt: (b, 0, 0)),
            pl.BlockSpec((None, kb, width), lambda b, i, pt: (b, 0, 0)),
            pl.BlockSpec((rows, 1), lambda b, i, pt: (0, 0)),
            pl.BlockSpec((kb, kb), lambda b, i, pt: (0, 0)),
        ] + [page_spec(r) for r in range(n_pg)] * 2,
        out_specs=pl.BlockSpec((n_q, width), lambda b, i, pt: (b, 0)),
        scratch_shapes=[pltpu.VMEM((rows, width), F32), pltpu.VMEM((rows, 1), F32)],
    )
    return pl.pallas_call(
        functools.partial(_sb_sample_kernel, n_q=n_q, n_pg=n_pg, kb=kb),
        grid_spec=grid_spec,
        out_shape=jax.ShapeDtypeStruct((t, width), F32),
        compiler_params=_cparams(("parallel", "arbitrary")),
        name="sb_sample",
    )(page_table, q, kn, vn, bias_col, _tri(kb), *([cache_k] * n_pg), *([cache_v] * n_pg))


def _pick_tile(t, pref):
    while t % pref:
        pref //= 2
    return pref


def kernel(x_prompt, x_sample, mem_prompt, cache_sb_k, cache_sb_v, page_table, cache_mem_k, cache_mem_v, state_ssm_re, state_ssm_im, norm_mix, norm_mlp, norm_mem, norm_final, w_in_ssm, ssm_lambda_re, ssm_lambda_im, ssm_log_step, ssm_b_re, ssm_b_im, ssm_c_re, ssm_c_im, ssm_d, ssm_w_glu, w_in_sb, sb_bias, w_mem_kv, w_out, w_up, w_down):
    n_p, seq_p, d_model = x_prompt.shape
    n_s, seq_s, _ = x_sample.shape
    depth = w_out.shape[0]
    n_mem = mem_prompt.shape[1]
    tok_w = ssm_d.shape[1]
    n_groups, n_state = ssm_lambda_re.shape[1:]
    sb_heads = tok_w // HEAD_DIM
    pool = cache_sb_k.shape[1]
    lc_p = 16
    nc_p = seq_p // lc_p

    xp = x_prompt.reshape(n_p * seq_p, d_model)
    xs = x_sample.reshape(n_s * seq_s, d_model)
    tp, ts = xp.shape[0], xs.shape[0]
    tm_p = _pick_tile(tp, 512)
    tm_s = _pick_tile(ts, 256)

    w_in_ssm_b = w_in_ssm.astype(BF16)
    w_in_sb_b = w_in_sb.astype(BF16)
    w_out_b = w_out.astype(BF16)
    w_up_b = w_up.astype(BF16)
    w_down_b = w_down.astype(BF16)
    w_glu_b = ssm_w_glu.astype(BF16)
    cache_k4 = cache_sb_k.reshape(cache_sb_k.shape[0], pool, PAGE_SIZE, tok_w)
    cache_v4 = cache_sb_v.reshape(cache_sb_v.shape[0], pool, PAGE_SIZE, tok_w)

    mem_k, mem_v = mem_kv_all(mem_prompt.reshape(n_p * n_mem, d_model), norm_mem, w_mem_kv.astype(BF16))

    ssm_re_p, ssm_im_p, ssm_re_s, ssm_im_s = [], [], [], []
    sb_k_p, sb_v_p, sb_k_s, sb_v_s = [], [], [], []
    y_p = y_s = None
    for i in range(depth):
        j = i // 2
        mk_s = cache_mem_k[i].reshape(n_s * n_mem, MEM_WIDTH)
        mv_s = cache_mem_v[i].reshape(n_s * n_mem, MEM_WIDTH)
        if i % 2 == 0:
            prm = (ssm_lambda_re[j], ssm_lambda_im[j], ssm_log_step[j], ssm_b_re[j], ssm_b_im[j],
                   ssm_c_re[j], ssm_c_im[j], ssm_d[j])
            mats_p = ssm_matrices(*prm, lc=lc_p, nc=nc_p)
            mats_s = ssm_matrices(*prm, lc=seq_s, nc=1)
            u_p, qm_p = norm_proj(xp, norm_mix[i], w_in_ssm_b[j], (tok_w, MEM_WIDTH), tm_p)
            u_s, qm_s = norm_proj(xs, norm_mix[i], w_in_ssm_b[j], (tok_w, MEM_WIDTH), tm_s)
            zero = jnp.zeros((n_p, n_groups, n_state), F32)
            yv_p, fr_p, fi_p = ssm_apply(u_p, _state_to_slabs(zero, 1), _state_to_slabs(zero, 1),
                                         mats_p, lc_p, nc_p, 1)
            yv_s, fr_s, fi_s = ssm_apply(u_s, _state_to_slabs(state_ssm_re[j].astype(F32), n_s),
                                         _state_to_slabs(state_ssm_im[j].astype(F32), n_s),
                                         mats_s, seq_s, 1, n_s)
            mix_p = glu(yv_p, w_glu_b[j], tm_p)
            mix_s = glu(yv_s, w_glu_b[j], tm_s)
            ssm_re_p.append(_slabs_to_state(fr_p, n_state))
            ssm_im_p.append(_slabs_to_state(fi_p, n_state))
            ssm_re_s.append(_slabs_to_state(fr_s, n_state))
            ssm_im_s.append(_slabs_to_state(fi_s, n_state))
        else:
            widths = (tok_w, tok_w, tok_w, MEM_WIDTH)
            q_p, k_p, v_p, qm_p = norm_proj(xp, norm_mix[i], w_in_sb_b[j], widths, tm_p)
            q_s, k_s, v_s, qm_s = norm_proj(xs, norm_mix[i], w_in_sb_b[j], widths, tm_s)
            mix_p = sb_prompt(q_p, k_p, v_p, sb_bias[j], n_p, 256)
            mix_s = sb_sample(q_s, k_s, v_s, cache_k4, cache_v4, j, page_table, sb_bias[j], seq_s)
            sb_k_p.append(k_p.reshape(n_p, seq_p, sb_heads, HEAD_DIM))
            sb_v_p.append(v_p.reshape(n_p, seq_p, sb_heads, HEAD_DIM))
            sb_k_s.append(k_s.reshape(n_s, seq_s, sb_heads, HEAD_DIM))
            sb_v_s.append(v_s.reshape(n_s, seq_s, sb_heads, HEAD_DIM))
        mo_p = mem_attn(qm_p, mem_k[i], mem_v[i], n_p, tm_p)
        mo_s = mem_attn(qm_s, mk_s, mv_s, n_s, seq_s)
        xp = out_proj(xp, mix_p, mo_p, w_out_b[i], tm_p)
        xs = out_proj(xs, mix_s, mo_s, w_out_b[i], tm_s)
        last = i == depth - 1
        xp, y_p = mlp(xp, norm_mlp[i], w_up_b[i], w_down_b[i], norm_final, _pick_tile(tp, 1024), 1024, last)
        xs, y_s = mlp(xs, norm_mlp[i], w_up_b[i], w_down_b[i], norm_final, tm_s, 1024, last)

    return (y_p.reshape(n_p, seq_p, d_model), y_s.reshape(n_s, seq_s, d_model),
            jnp.stack(ssm_re_p), jnp.stack(ssm_im_p), jnp.stack(ssm_re_s), jnp.stack(ssm_im_s),
            jnp.stack(sb_k_p), jnp.stack(sb_v_p), jnp.stack(sb_k_s), jnp.stack(sb_v_s),
            mem_k.reshape(depth, n_p, n_mem, MEM_HEADS, HEAD_DIM),
            mem_v.reshape(depth, n_p, n_mem, MEM_HEADS, HEAD_DIM))
```

```python
import functools
import math

import jax
import jax.numpy as jnp
from jax import lax
from jax.experimental import pallas as pl
from jax.experimental.pallas import tpu as pltpu

F32 = jnp.float32
BF16 = jnp.bfloat16

RMS_EPS = 1e-6
HEAD_DIM = 64
ATTN_SCALE = 1.0 / math.sqrt(HEAD_DIM)
MEM_HEADS = 4
MEM_WIDTH = MEM_HEADS * HEAD_DIM
SSM_GROUP_CH = 16
SSM_STATE = 64
PAGE_SIZE = 128

LANES = 128
MXU_DIM = 256
SLAB_GROUPS = LANES // SSM_GROUP_CH
SLAB_STATE = SLAB_GROUPS * SSM_STATE
VMEM_LIMIT = 56 * 1024 * 1024


def _cparams(sem):
    return pltpu.CompilerParams(dimension_semantics=sem, vmem_limit_bytes=VMEM_LIMIT)


def _rmsnorm(x, g):
    ms = jnp.mean(x * x, axis=-1, keepdims=True)
    return x * lax.rsqrt(ms + RMS_EPS) * g


def _dot(a, b):
    return jnp.dot(a, b, preferred_element_type=F32)


def _dot_nt(a, b):
    return lax.dot_general(a, b, (((1,), (1,)), ((), ())), preferred_element_type=F32)


def _cmul(ar, ai, br, bi):
    return ar * br - ai * bi, ar * bi + ai * br


def _norm_proj_kernel(x_ref, g_ref, w_ref, *o_refs, widths):
    h = _rmsnorm(x_ref[...], g_ref[...]).astype(BF16)
    r = _dot(h, w_ref[...])
    off = 0
    for o_ref, wd in zip(o_refs, widths):
        o_ref[...] = r[:, off:off + wd]
        off += wd


def norm_proj(x, g, w_bf16, widths, tm):
    t, d = x.shape
    n = w_bf16.shape[1]
    assert sum(widths) == n and t % tm == 0
    return pl.pallas_call(
        functools.partial(_norm_proj_kernel, widths=widths),
        grid=(t // tm,),
        in_specs=[
            pl.BlockSpec((tm, d), lambda i: (i, 0)),
            pl.BlockSpec((1, d), lambda i: (0, 0)),
            pl.BlockSpec((d, n), lambda i: (0, 0)),
        ],
        out_specs=[pl.BlockSpec((tm, wd), lambda i: (i, 0)) for wd in widths],
        out_shape=[jax.ShapeDtypeStruct((t, wd), F32) for wd in widths],
        compiler_params=_cparams(("parallel",)),
        name="norm_proj",
    )(x, g.reshape(1, d), w_bf16)


def _sb_proj_kernel(x_ref, g_ref, wq_ref, wkvt_ref, q_ref, qm_ref, kt_ref, vt_ref, *, tok_w):
    h = _rmsnorm(x_ref[...], g_ref[...]).astype(BF16)
    r = _dot(h, wq_ref[...])
    q_ref[...] = r[:, :tok_w]
    qm_ref[...] = r[:, tok_w:]
    kv = _dot_nt(wkvt_ref[...], h)
    kt_ref[...] = kv[:tok_w]
    vt_ref[...] = kv[tok_w:]


def sb_proj(x, g, wq_bf16, wkvt_bf16, n_seq, tm):
    t, d = x.shape
    seq_len = t // n_seq
    nt = seq_len // tm
    tok_w = wkvt_bf16.shape[0] // 2
    nq = wq_bf16.shape[1]
    return pl.pallas_call(
        functools.partial(_sb_proj_kernel, tok_w=tok_w),
        grid=(n_seq, nt),
        in_specs=[
            pl.BlockSpec((tm, d), lambda b, i: (b * nt + i, 0)),
            pl.BlockSpec((1, d), lambda b, i: (0, 0)),
            pl.BlockSpec((d, nq), lambda b, i: (0, 0)),
            pl.BlockSpec((2 * tok_w, d), lambda b, i: (0, 0)),
        ],
        out_specs=[
            pl.BlockSpec((tm, tok_w), lambda b, i: (b * nt + i, 0)),
            pl.BlockSpec((tm, nq - tok_w), lambda b, i: (b * nt + i, 0)),
            pl.BlockSpec((None, tok_w, tm), lambda b, i: (b, 0, i)),
            pl.BlockSpec((None, tok_w, tm), lambda b, i: (b, 0, i)),
        ],
        out_shape=[
            jax.ShapeDtypeStruct((t, tok_w), F32),
            jax.ShapeDtypeStruct((t, nq - tok_w), F32),
            jax.ShapeDtypeStruct((n_seq, tok_w, seq_len), F32),
            jax.ShapeDtypeStruct((n_seq, tok_w, seq_len), F32),
        ],
        compiler_params=_cparams(("parallel", "parallel")),
        name="sb_proj",
    )(x, g.reshape(1, d), wq_bf16, wkvt_bf16)


def _mem_kv_kernel(x_ref, g_ref, wt_ref, k_ref, v_ref, h_scr):
    b = pl.program_id(1)

    @pl.when(pl.program_id(0) == 0)
    def _():
        h_scr[b] = _rmsnorm(x_ref[...], g_ref[...]).astype(BF16)

    r = _dot_nt(wt_ref[...], h_scr[b])
    k_ref[...] = r[:MEM_WIDTH]
    v_ref[...] = r[MEM_WIDTH:]


def mem_kv_all(mem, g, wt_bf16):
    n_seq, n_mem, d = mem.shape
    depth = wt_bf16.shape[0]
    return pl.pallas_call(
        _mem_kv_kernel,
        grid=(depth, n_seq),
        in_specs=[
            pl.BlockSpec((None, n_mem, d), lambda l, b: (b, 0, 0)),
            pl.BlockSpec((1, d), lambda l, b: (0, 0)),
            pl.BlockSpec((None, 2 * MEM_WIDTH, d), lambda l, b: (l, 0, 0)),
        ],
        out_specs=[pl.BlockSpec((None, None, MEM_WIDTH, n_mem), lambda l, b: (l, b, 0, 0))] * 2,
        out_shape=[jax.ShapeDtypeStruct((depth, n_seq, MEM_WIDTH, n_mem), F32)] * 2,
        scratch_shapes=[pltpu.VMEM((n_seq, n_mem, d), BF16)],
        compiler_params=_cparams(("arbitrary", "arbitrary")),
        name="mem_kv",
    )(mem, g.reshape(1, d), wt_bf16)


def _mlp_kernel(x_ref, g_ref, wu_ref, wd_ref, gf_ref, *refs, nf, final_norm):
    if final_norm:
        o_ref, y_ref, h_scr, acc_scr = refs
    else:
        o_ref, h_scr, acc_scr = refs
    f = pl.program_id(1)

    @pl.when(f == 0)
    def _():
        h_scr[...] = _rmsnorm(x_ref[...], g_ref[...]).astype(BF16)
        acc_scr[...] = jnp.zeros_like(acc_scr)

    a = _dot(h_scr[...], wu_ref[...])
    a = jnp.square(jnp.maximum(a, 0.0)).astype(BF16)
    acc_scr[...] += _dot(a, wd_ref[...])

    @pl.when(f == nf - 1)
    def _():
        xn = x_ref[...] + acc_scr[...]
        o_ref[...] = xn
        if final_norm:
            y_ref[...] = _rmsnorm(xn, gf_ref[...])


def mlp(x, g, wu_bf16, wd_bf16, g_final, tm, tf, final_norm):
    t, d = x.shape
    ff = wu_bf16.shape[1]
    nf = ff // tf
    n_out = 2 if final_norm else 1
    outs = pl.pallas_call(
        functools.partial(_mlp_kernel, nf=nf, final_norm=final_norm),
        grid=(t // tm, nf),
        in_specs=[
            pl.BlockSpec((tm, d), lambda i, f: (i, 0)),
            pl.BlockSpec((1, d), lambda i, f: (0, 0)),
            pl.BlockSpec((d, tf), lambda i, f: (0, f)),
            pl.BlockSpec((tf, d), lambda i, f: (f, 0)),
            pl.BlockSpec((1, d), lambda i, f: (0, 0)),
        ],
        out_specs=[pl.BlockSpec((tm, d), lambda i, f: (i, 0))] * n_out,
        out_shape=[jax.ShapeDtypeStruct((t, d), F32)] * n_out,
        scratch_shapes=[pltpu.VMEM((tm, d), BF16), pltpu.VMEM((tm, d), F32)],
        compiler_params=_cparams(("parallel", "arbitrary")),
        name="mlp",
    )(x, g.reshape(1, d), wu_bf16, wd_bf16, g_final.reshape(1, d))
    return outs if final_norm else (outs[0], None)


def _glu_kernel(y_ref, w_ref, o_ref):
    g = jax.nn.gelu(y_ref[...])
    gm = _dot(g.astype(BF16), w_ref[...])
    o_ref[...] = g * jax.nn.sigmoid(gm)


def glu(y, w_bf16, tm):
    t, n = y.shape
    return pl.pallas_call(
        _glu_kernel,
        grid=(t // tm,),
        in_specs=[pl.BlockSpec((tm, n), lambda i: (i, 0)), pl.BlockSpec((n, n), lambda i: (0, 0))],
        out_specs=pl.BlockSpec((tm, n), lambda i: (i, 0)),
        out_shape=jax.ShapeDtypeStruct((t, n), F32),
        compiler_params=_cparams(("parallel",)),
        name="glu",
    )(y, w_bf16)


def _mem_attn_kernel(q_ref, kt_ref, vt_ref, o_ref):
    q = q_ref[...] * ATTN_SCALE
    kt = kt_ref[...].astype(BF16)
    vt = vt_ref[...].astype(BF16)
    lane_head = lax.broadcasted_iota(jnp.int32, (1, MEM_WIDTH), 1) // HEAD_DIM
    out = jnp.zeros(q.shape, F32)
    for h in range(MEM_HEADS):
        m = lane_head == h
        s = _dot(jnp.where(m, q, 0.0).astype(BF16), kt)
        p = jnp.exp(s - jnp.max(s, axis=-1, keepdims=True))
        p = p / jnp.sum(p, axis=-1, keepdims=True)
        out = out + jnp.where(m, _dot_nt(p.astype(BF16), vt), 0.0)
    o_ref[...] = out


def mem_attn(qm, mkt, mvt, layer, n_seq, tm):
    t = qm.shape[0]
    seq_len = t // n_seq
    n_mem = mkt.shape[-1]
    nt = seq_len // tm
    kv_spec = pl.BlockSpec((None, None, MEM_WIDTH, n_mem), lambda b, i: (layer, b, 0, 0))
    return pl.pallas_call(
        _mem_attn_kernel,
        grid=(n_seq, nt),
        in_specs=[pl.BlockSpec((tm, MEM_WIDTH), lambda b, i: (b * nt + i, 0)), kv_spec, kv_spec],
        out_specs=pl.BlockSpec((tm, MEM_WIDTH), lambda b, i: (b * nt + i, 0)),
        out_shape=jax.ShapeDtypeStruct((t, MEM_WIDTH), F32),
        compiler_params=_cparams(("parallel", "parallel")),
        name="mem_attn",
    )(qm, mkt, mvt)


def _out_proj_kernel(x_ref, mix_ref, mem_ref, w1_ref, w2_ref, o_ref):
    acc = _dot(mix_ref[...].astype(BF16), w1_ref[...])
    acc = acc + _dot(mem_ref[...].astype(BF16), w2_ref[...])
    o_ref[...] = x_ref[...] + acc


def out_proj(x, mix, mem_o, w_bf16, tm):
    t, d = x.shape
    n1 = mix.shape[1]
    n2 = mem_o.shape[1]
    return pl.pallas_call(
        _out_proj_kernel,
        grid=(t // tm,),
        in_specs=[
            pl.BlockSpec((tm, d), lambda i: (i, 0)),
            pl.BlockSpec((tm, n1), lambda i: (i, 0)),
            pl.BlockSpec((tm, n2), lambda i: (i, 0)),
            pl.BlockSpec((n1, d), lambda i: (0, 0)),
            pl.BlockSpec((n2, d), lambda i: (0, 0)),
        ],
        out_specs=pl.BlockSpec((tm, d), lambda i: (i, 0)),
        out_shape=jax.ShapeDtypeStruct((t, d), F32),
        compiler_params=_cparams(("parallel",)),
        name="out_proj",
    )(x, mix, mem_o, w_bf16[:n1], w_bf16[n1:])


def _ssm_prep_kernel(lr_ref, li_ref, ls_ref, lrc_ref, lic_ref, lsc_ref, btr_ref, bti_ref, ctr_ref, cti_ref,
                     m_ref, bc_ref, cc_ref, apr_ref, api_ref,
                     hc_scr, d_scr, pr_scr, pi_scr, bbr_scr, bbi_scr, *, lc, n_pow):
    s = pl.program_id(1)

    def discretise(lam_re, lam_im, log_step):
        a_re = jnp.minimum(lam_re, -1e-4)
        step = jnp.exp(log_step)
        mag = jnp.exp(step * a_re)
        return a_re, lam_im, mag * jnp.cos(step * lam_im), mag * jnp.sin(step * lam_im)

    @pl.when(s == 0)
    def _():
        a_re, a_im, lbr, lbi = discretise(lr_ref[...], li_ref[...], ls_ref[...])
        den = a_re * a_re + a_im * a_im
        nr, ni = _cmul(lbr - 1.0, lbi, a_re, -a_im)
        bbr, bbi = _cmul(nr / den, ni / den, btr_ref[...], bti_ref[...])
        bbr_scr[...] = bbr
        bbi_scr[...] = bbi
        pr, pi = jnp.ones_like(lbr), jnp.zeros_like(lbi)
        for j in range(lc + 1):
            pr_scr[j] = pr
            pi_scr[j] = pi
            if j == lc:
                ar, ai = pr, pi
                for k in range(n_pow):
                    apr_ref[k:k + 1, :] = ar
                    api_ref[k:k + 1, :] = ai
                    ar, ai = _cmul(ar, ai, ar, ai)
            pr, pi = _cmul(pr, pi, lbr, lbi)
        _, _, lbrc, lbic = discretise(lrc_ref[...], lic_ref[...], lsc_ref[...])
        shape = ctr_ref.shape
        lbrc = jnp.broadcast_to(lbrc, shape)
        lbic = jnp.broadcast_to(lbic, shape)
        qr, qi = ctr_ref[...], cti_ref[...]
        bcat = jnp.concatenate([bbr, bbi], axis=1)
        for j in range(lc + 1):
            hj = jnp.concatenate([qr, -qi], axis=0)
            hc_scr[j] = hj
            if j < lc:
                d_scr[j] = jnp.dot(bcat, hj, precision=lax.Precision.HIGHEST, preferred_element_type=F32)
            qr, qi = _cmul(qr, qi, lbrc, lbic)

    pr, pi = pr_scr[s], pi_scr[s]
    gr, gi = _cmul(bbr_scr[...], bbi_scr[...], pr, pi)
    bc_ref[...] = jnp.concatenate([gr, gi], axis=1).astype(BF16)
    cc_ref[...] = hc_scr[s + 1].astype(BF16)
    for t in range(lc):
        blk = d_scr[jnp.maximum(t - s, 0)]
        m_ref[:, t * LANES:(t + 1) * LANES] = jnp.where(t >= s, blk, 0.0).astype(BF16)


def ssm_operators(lam_re, lam_im, log_step, b_re, b_im, c_re, c_im, lc, n_pow):
    n_g, n_p, n_c = b_re.shape
    n_slab = n_g // SLAB_GROUPS
    same = jnp.eye(SLAB_GROUPS, dtype=bool)

    def rows(x):
        return x.astype(F32).reshape(n_slab, 1, SLAB_STATE)

    def cols(x):
        return x.astype(F32).reshape(n_slab, SLAB_STATE, 1)

    def bt_blocks(b):
        bt = jnp.transpose(b.astype(F32), (0, 2, 1)).reshape(n_slab, SLAB_GROUPS, n_c, 1, n_p)
        bt = jnp.where(same[None, :, None, :, None], bt, 0.0)
        return bt.reshape(n_slab, LANES, SLAB_STATE)

    def ct_blocks(c):
        ct = jnp.transpose(c.astype(F32), (0, 2, 1)).reshape(n_slab, SLAB_GROUPS, n_p, 1, n_c)
        ct = jnp.where(same[None, :, None, :, None], ct, 0.0)
        return ct.reshape(n_slab, SLAB_STATE, LANES)

    step_gp = jnp.broadcast_to(log_step.astype(F32)[:, None], (n_g, n_p))
    row_spec = pl.BlockSpec((None, 1, SLAB_STATE), lambda g, s: (g, 0, 0))
    col_spec = pl.BlockSpec((None, SLAB_STATE, 1), lambda g, s: (g, 0, 0))
    bt_spec = pl.BlockSpec((None, LANES, SLAB_STATE), lambda g, s: (g, 0, 0))
    ct_spec = pl.BlockSpec((None, SLAB_STATE, LANES), lambda g, s: (g, 0, 0))
    pow_spec = pl.BlockSpec((None, n_pow, SLAB_STATE), lambda g, s: (g, 0, 0))
    kdim = lc * LANES
    return pl.pallas_call(
        functools.partial(_ssm_prep_kernel, lc=lc, n_pow=n_pow),
        grid=(n_slab, lc),
        in_specs=[row_spec] * 3 + [col_spec] * 3 + [bt_spec] * 2 + [ct_spec] * 2,
        out_specs=[
            pl.BlockSpec((None, LANES, kdim), lambda g, s: (g, s, 0)),
            pl.BlockSpec((None, LANES, 2 * SLAB_STATE), lambda g, s: (g, lc - 1 - s, 0)),
            pl.BlockSpec((None, 2 * SLAB_STATE, LANES), lambda g, s: (g, 0, s)),
            pow_spec, pow_spec,
        ],
        out_shape=[
            jax.ShapeDtypeStruct((n_slab, kdim, kdim), BF16),
            jax.ShapeDtypeStruct((n_slab, kdim, 2 * SLAB_STATE), BF16),
            jax.ShapeDtypeStruct((n_slab, 2 * SLAB_STATE, kdim), BF16),
            jax.ShapeDtypeStruct((n_slab, n_pow, SLAB_STATE), F32),
            jax.ShapeDtypeStruct((n_slab, n_pow, SLAB_STATE), F32),
        ],
        scratch_shapes=[
            pltpu.VMEM((lc + 1, 2 * SLAB_STATE, LANES), F32),
            pltpu.VMEM((lc, LANES, LANES), F32),
            pltpu.VMEM((lc + 1, 1, SLAB_STATE), F32),
            pltpu.VMEM((lc + 1, 1, SLAB_STATE), F32),
            pltpu.VMEM((LANES, SLAB_STATE), F32),
            pltpu.VMEM((LANES, SLAB_STATE), F32),
        ],
        compiler_params=_cparams(("parallel", "arbitrary")),
        name="ssm_prep",
    )(rows(lam_re), rows(lam_im), rows(step_gp), cols(lam_re), cols(lam_im), cols(step_gp),
      bt_blocks(b_re), bt_blocks(b_im), ct_blocks(c_re), ct_blocks(c_im))


def _ssm_kernel(u_ref, x0r_ref, x0i_ref, m_ref, bc_ref, cc_ref, d_ref, apr_ref, api_ref,
                y_ref, xr_ref, xi_ref, *, lc, nc):
    rows = u_ref.shape[0] // lc
    us = [u_ref[pl.ds(s, rows, stride=lc), :] for s in range(lc)]
    ucat = jnp.concatenate([u.astype(BF16) for u in us], axis=1)
    y = _dot(ucat, m_ref[...])
    v = _dot(ucat, bc_ref[...])
    vr, vi = v[:, :SLAB_STATE], v[:, SLAB_STATE:]
    x0r, x0i = x0r_ref[...], x0i_ref[...]
    ar, ai = apr_ref[0:1, :], api_ref[0:1, :]
    inj_r, inj_i = _cmul(ar, ai, x0r, x0i)
    if nc == 1:
        xr, xi = vr + inj_r, vi + inj_i
        pr, pi = x0r, x0i
        xr_ref[...] = xr
        xi_ref[...] = xi
    else:
        row = lax.broadcasted_iota(jnp.int32, (rows, 1), 0)
        vr = vr + jnp.where(row == 0, inj_r, 0.0)
        vi = vi + jnp.where(row == 0, inj_i, 0.0)
        for k in range(nc.bit_length() - 1):
            sh = 1 << k
            sr = jnp.where(row >= sh, pltpu.roll(vr, sh, 0), 0.0)
            si = jnp.where(row >= sh, pltpu.roll(vi, sh, 0), 0.0)
            dr, di = _cmul(apr_ref[k:k + 1, :], api_ref[k:k + 1, :], sr, si)
            vr, vi = vr + dr, vi + di
        xr_ref[...] = vr[rows - 1:rows, :]
        xi_ref[...] = vi[rows - 1:rows, :]
        pr = jnp.where(row >= 1, pltpu.roll(vr, 1, 0), x0r)
        pi = jnp.where(row >= 1, pltpu.roll(vi, 1, 0), x0i)
    prev = jnp.concatenate([pr.astype(BF16), pi.astype(BF16)], axis=1)
    y = y + _dot(prev, cc_ref[...])
    d = d_ref[...]
    for s in range(lc):
        y_ref[pl.ds(s, rows, stride=lc), :] = y[:, s * LANES:(s + 1) * LANES] + d * us[s]


def ssm_apply(u, x0r, x0i, ops, d_skip, lc, nc, nb):
    m, bc, cc, apr, api = ops
    t, width = u.shape
    n_slab = width // LANES
    tok = nb * nc * lc
    n_blk = t // tok
    kdim = lc * LANES
    n_pow = apr.shape[1]
    state_spec = pl.BlockSpec((None, None, nb, SLAB_STATE), lambda s, b: (s, b, 0, 0))
    pow_spec = pl.BlockSpec((None, n_pow, SLAB_STATE), lambda s, b: (s, 0, 0))
    state_shape = jax.ShapeDtypeStruct((n_slab, n_blk, nb, SLAB_STATE), F32)
    return pl.pallas_call(
        functools.partial(_ssm_kernel, lc=lc, nc=nc),
        grid=(n_slab, n_blk),
        in_specs=[
            pl.BlockSpec((tok, LANES), lambda s, b: (b, s)),
            state_spec, state_spec,
            pl.BlockSpec((None, kdim, kdim), lambda s, b: (s, 0, 0)),
            pl.BlockSpec((None, kdim, 2 * SLAB_STATE), lambda s, b: (s, 0, 0)),
            pl.BlockSpec((None, 2 * SLAB_STATE, kdim), lambda s, b: (s, 0, 0)),
            pl.BlockSpec((None, 1, LANES), lambda s, b: (s, 0, 0)),
            pow_spec, pow_spec,
        ],
        out_specs=[pl.BlockSpec((tok, LANES), lambda s, b: (b, s)), state_spec, state_spec],
        out_shape=[jax.ShapeDtypeStruct((t, width), F32), state_shape, state_shape],
        compiler_params=_cparams(("arbitrary", "arbitrary")),
        name="ssm",
    )(u, x0r, x0i, m, bc, cc, d_skip.astype(F32).reshape(n_slab, 1, LANES), apr, api)


def _state_to_slabs(x, nb):
    n_seq, n_g, n_p = x.shape
    n_slab = n_g // SLAB_GROUPS
    x = x.reshape(n_seq, n_slab, SLAB_STATE)
    return jnp.transpose(x, (1, 0, 2)).reshape(n_slab, n_seq // nb, nb, SLAB_STATE)


def _slabs_to_state(x, n_p):
    n_slab, n_blk, nb, _ = x.shape
    x = jnp.transpose(x.reshape(n_slab, n_blk * nb, SLAB_STATE), (1, 0, 2))
    return x.reshape(n_blk * nb, n_slab * SLAB_GROUPS, n_p)


def _sb_block(qh, kt_blk, vt_blk, bias, tri, carry, mask):
    n_keys = kt_blk.shape[1]
    n_tri = tri.shape[0]
    z = _dot(qh, kt_blk) + bias
    lp = jnp.log(1.0 + jnp.exp(-jnp.abs(z)))
    l1m = -jnp.maximum(z, 0.0) - lp
    if mask is not None:
        l1m = jnp.where(mask, l1m, 0.0)
    lb = l1m.astype(BF16)
    parts = []
    for c in reversed(range(n_keys // n_tri)):
        sl = slice(c * n_tri, (c + 1) * n_tri)
        parts.insert(0, _dot(lb[:, sl], tri) + carry)
        carry = carry + jnp.sum(l1m[:, sl], axis=-1, keepdims=True)
    suffix = parts[0] if len(parts) == 1 else jnp.concatenate(parts, axis=1)
    w = jnp.exp(z + l1m + suffix)
    if mask is not None:
        w = jnp.where(mask, w, 0.0)
    return _dot_nt(w.astype(BF16), vt_blk), carry


def _sb_prompt_kernel(bias_ref, q_ref, kt_ref, vt_ref, tri_ref, o_ref, *, tq):
    pair = pl.program_id(1)
    i = pl.program_id(2)
    n_h = LANES // HEAD_DIM
    q = q_ref[...] * ATTN_SCALE
    tri = tri_ref[...]
    lane_head = lax.broadcasted_iota(jnp.int32, (1, LANES), 1) // HEAD_DIM
    qpos = lax.broadcasted_iota(jnp.int32, (tq, tq), 0)
    kpos = lax.broadcasted_iota(jnp.int32, (tq, tq), 1)
    diag_mask = kpos < qpos
    qhs = [jnp.where(lane_head == hh, q, 0.0).astype(BF16) for hh in range(n_h)]
    biases = [bias_ref[pair * n_h + hh] for hh in range(n_h)]

    def blocks(start, carries, mask):
        kt = kt_ref[:, pl.ds(start, tq)].astype(BF16)
        vt = vt_ref[:, pl.ds(start, tq)].astype(BF16)
        return [_sb_block(qhs[hh], kt, vt, biases[hh], tri, carries[hh], mask) for hh in range(n_h)]

    res = blocks(pl.multiple_of(i * tq, tq), [jnp.zeros((tq, 1), F32)] * n_h, diag_mask)

    def body(it, carry):
        res = blocks(pl.multiple_of((i - 1 - it) * tq, tq), [c for _, c in carry], None)
        return tuple((o + ob, c) for (o, _), (ob, c) in zip(carry, res))

    res = lax.fori_loop(0, i, body, tuple(res))
    out = res[0][0]
    for hh in range(1, n_h):
        out = jnp.where(lane_head == hh, res[hh][0], out)
    o_ref[...] = out


def _tri(n):
    r = jnp.arange(n)
    return (r[:, None] > r[None, :]).astype(BF16)


def sb_prompt(q, kt, vt, bias, tq):
    t, width = q.shape
    n_seq, _, seq_len = kt.shape
    nq = seq_len // tq
    n_pair = width // LANES
    kv_spec = pl.BlockSpec((None, LANES, seq_len), lambda b, p, i: (b, p, 0))
    return pl.pallas_call(
        functools.partial(_sb_prompt_kernel, tq=tq),
        grid=(n_seq, n_pair, nq),
        in_specs=[
            pl.BlockSpec(memory_space=pltpu.SMEM),
            pl.BlockSpec((tq, LANES), lambda b, p, i: (b * nq + i, p)),
            kv_spec, kv_spec,
            pl.BlockSpec((MXU_DIM, MXU_DIM), lambda b, p, i: (0, 0)),
        ],
        out_specs=pl.BlockSpec((tq, LANES), lambda b, p, i: (b * nq + i, p)),
        out_shape=jax.ShapeDtypeStruct((t, width), F32),
        compiler_params=_cparams(("parallel", "parallel", "arbitrary")),
        name="sb_prompt",
    )(bias.astype(F32), q, kt, vt, _tri(MXU_DIM))


def _sb_sample_kernel(pt_ref, q_ref, knt_ref, vnt_ref, bias_ref, tri_ref, *refs, n_q, n_pg):
    k_refs = refs[:n_pg]
    v_refs = refs[n_pg:2 * n_pg]
    o_ref = refs[2 * n_pg]
    acc_scr, carry_scr = refs[2 * n_pg + 1:]
    i = pl.program_id(1)
    width = q_ref.shape[1]
    n_heads = width // HEAD_DIM
    rows = n_heads * n_q
    row = lax.broadcasted_iota(jnp.int32, (rows, 1), 0)
    head_mask = (lax.broadcasted_iota(jnp.int32, (1, width), 1) // HEAD_DIM) == (row // n_q)
    q = q_ref[...] * ATTN_SCALE
    qbd = jnp.where(head_mask, jnp.concatenate([q] * n_heads, axis=0), 0.0).astype(BF16)
    bias = bias_ref[...]
    tri = tri_ref[...]

    @pl.when(i == 0)
    def _():
        kpos = lax.broadcasted_iota(jnp.int32, (rows, PAGE_SIZE), 1)
        mask = kpos < (row % n_q)
        o, c = _sb_block(qbd, knt_ref[...].astype(BF16), vnt_ref[...].astype(BF16), bias,
                         tri[:PAGE_SIZE, :PAGE_SIZE], jnp.zeros((rows, 1), F32), mask)
        acc_scr[...] = o
        carry_scr[...] = c

    acc = acc_scr[...]
    carry = carry_scr[...]
    per_blk = MXU_DIM // PAGE_SIZE
    for j in range(n_pg // per_blk):
        pages = [j * per_blk + r for r in reversed(range(per_blk))]
        kt = jnp.concatenate([k_refs[p][...] for p in pages], axis=1).astype(BF16)
        vt = jnp.concatenate([v_refs[p][...] for p in pages], axis=1).astype(BF16)
        o, carry = _sb_block(qbd, kt, vt, bias, tri, carry, None)
        acc = acc + o
    acc_scr[...] = acc
    carry_scr[...] = carry

    @pl.when(i == pl.num_programs(1) - 1)
    def _():
        a = jnp.where(head_mask, acc, 0.0)
        out = a[0:n_q]
        for h in range(1, n_heads):
            out = out + a[h * n_q:(h + 1) * n_q]
        o_ref[...] = out


def sb_sample(q, knt, vnt, cache_kt, cache_vt, layer, page_table, bias, n_q, n_pg=8):
    t, width = q.shape
    n_seq = t // n_q
    n_pages = page_table.shape[1]
    rows = (width // HEAD_DIM) * n_q
    bias_col = jnp.repeat(bias.astype(F32), n_q).reshape(rows, 1)

    def page_spec(r):
        return pl.BlockSpec((None, None, width, PAGE_SIZE),
                            lambda b, i, pt: (layer, pt[b, n_pages - 1 - (i * n_pg + r)], 0, 0))

    new_spec = pl.BlockSpec((None, width, PAGE_SIZE), lambda b, i, pt: (b, 0, 0))
    grid_spec = pltpu.PrefetchScalarGridSpec(
        num_scalar_prefetch=1,
        grid=(n_seq, n_pages // n_pg),
        in_specs=[
            pl.BlockSpec((n_q, width), lambda b, i, pt: (b, 0)),
            new_spec, new_spec,
            pl.BlockSpec((rows, 1), lambda b, i, pt: (0, 0)),
            pl.BlockSpec((MXU_DIM, MXU_DIM), lambda b, i, pt: (0, 0)),
        ] + [page_spec(r) for r in range(n_pg)] * 2,
        out_specs=pl.BlockSpec((n_q, width), lambda b, i, pt: (b, 0)),
        scratch_shapes=[pltpu.VMEM((rows, width), F32), pltpu.VMEM((rows, 1), F32)],
    )
    return pl.pallas_call(
        functools.partial(_sb_sample_kernel, n_q=n_q, n_pg=n_pg),
        grid_spec=grid_spec,
        out_shape=jax.ShapeDtypeStruct((t, width), F32),
        compiler_params=_cparams(("parallel", "arbitrary")),
        name="sb_sample",
    )(page_table, q, knt, vnt, bias_col, _tri(MXU_DIM), *([cache_kt] * n_pg), *([cache_vt] * n_pg))


def _pick_tile(t, pref):
    while t % pref:
        pref //= 2
    return pref


def _feature_major_pages(cache):
    n_l, pool, page, heads, d = cache.shape
    return jnp.transpose(cache, (0, 1, 3, 4, 2)).reshape(n_l, pool, heads * d, page)


def _new_tokens_feature_major(x, n_seq):
    n_q = x.shape[0] // n_seq
    xt = jnp.transpose(x.reshape(n_seq, n_q, x.shape[1]), (0, 2, 1))
    return jnp.pad(xt, ((0, 0), (0, 0), (0, PAGE_SIZE - n_q)))


def kernel(x_prompt, x_sample, mem_prompt, cache_sb_k, cache_sb_v, page_table, cache_mem_k, cache_mem_v, state_ssm_re, state_ssm_im, norm_mix, norm_mlp, norm_mem, norm_final, w_in_ssm, ssm_lambda_re, ssm_lambda_im, ssm_log_step, ssm_b_re, ssm_b_im, ssm_c_re, ssm_c_im, ssm_d, ssm_w_glu, w_in_sb, sb_bias, w_mem_kv, w_out, w_up, w_down):
    n_p, seq_p, d_model = x_prompt.shape
    n_s, seq_s, _ = x_sample.shape
    depth = w_out.shape[0]
    n_mem = mem_prompt.shape[1]
    tok_w = ssm_d.shape[1]
    n_groups, n_state = ssm_lambda_re.shape[1:]
    sb_heads = tok_w // HEAD_DIM
    lc_p = 16
    nc_p = seq_p // lc_p

    xp = x_prompt.reshape(n_p * seq_p, d_model)
    xs = x_sample.reshape(n_s * seq_s, d_model)
    tp, ts = xp.shape[0], xs.shape[0]
    tm_p = _pick_tile(seq_p, 512)
    tm_s = _pick_tile(ts, 256)

    w_in_ssm_b = w_in_ssm.astype(BF16)
    w_in_sb_b = w_in_sb.astype(BF16)
    wq_sb_b = jnp.concatenate([w_in_sb_b[:, :, :tok_w], w_in_sb_b[:, :, 3 * tok_w:]], axis=2)
    wkvt_sb_b = jnp.transpose(w_in_sb_b[:, :, tok_w:3 * tok_w], (0, 2, 1))
    w_out_b = w_out.astype(BF16)
    w_up_b = w_up.astype(BF16)
    w_down_b = w_down.astype(BF16)
    w_glu_b = ssm_w_glu.astype(BF16)
    cache_kt = _feature_major_pages(cache_sb_k)
    cache_vt = _feature_major_pages(cache_sb_v)
    n_lm, _, _, mem_heads, _ = cache_mem_k.shape
    mem_kt_s = jnp.transpose(cache_mem_k, (0, 1, 3, 4, 2)).reshape(n_lm, n_s, MEM_WIDTH, n_mem)
    mem_vt_s = jnp.transpose(cache_mem_v, (0, 1, 3, 4, 2)).reshape(n_lm, n_s, MEM_WIDTH, n_mem)

    mem_kt, mem_vt = mem_kv_all(mem_prompt, norm_mem, jnp.transpose(w_mem_kv, (0, 2, 1)).astype(BF16))

    ssm_re_p, ssm_im_p, ssm_re_s, ssm_im_s = [], [], [], []
    sb_k_p, sb_v_p, sb_k_s, sb_v_s = [], [], [], []
    y_p = y_s = None
    for i in range(depth):
        j = i // 2
        if i % 2 == 0:
            prm = (ssm_lambda_re[j], ssm_lambda_im[j], ssm_log_step[j], ssm_b_re[j], ssm_b_im[j],
                   ssm_c_re[j], ssm_c_im[j])
            ops_p = ssm_operators(*prm, lc=lc_p, n_pow=nc_p.bit_length() - 1)
            ops_s = ssm_operators(*prm, lc=seq_s, n_pow=1)
            u_p, qm_p = norm_proj(xp, norm_mix[i], w_in_ssm_b[j], (tok_w, MEM_WIDTH), tm_p)
            u_s, qm_s = norm_proj(xs, norm_mix[i], w_in_ssm_b[j], (tok_w, MEM_WIDTH), tm_s)
            zero = jnp.zeros((n_p, n_groups, n_state), F32)
            yv_p, fr_p, fi_p = ssm_apply(u_p, _state_to_slabs(zero, 1), _state_to_slabs(zero, 1),
                                         ops_p, ssm_d[j], lc_p, nc_p, 1)
            yv_s, fr_s, fi_s = ssm_apply(u_s, _state_to_slabs(state_ssm_re[j].astype(F32), n_s),
                                         _state_to_slabs(state_ssm_im[j].astype(F32), n_s),
                                         ops_s, ssm_d[j], seq_s, 1, n_s)
            mix_p = glu(yv_p, w_glu_b[j], tm_p)
            mix_s = glu(yv_s, w_glu_b[j], tm_s)
            ssm_re_p.append(_slabs_to_state(fr_p, n_state))
            ssm_im_p.append(_slabs_to_state(fi_p, n_state))
            ssm_re_s.append(_slabs_to_state(fr_s, n_state))
            ssm_im_s.append(_slabs_to_state(fi_s, n_state))
        else:
            q_p, qm_p, kt_p, vt_p = sb_proj(xp, norm_mix[i], wq_sb_b[j], wkvt_sb_b[j], n_p, tm_p)
            q_s, k_s, v_s, qm_s = norm_proj(xs, norm_mix[i], w_in_sb_b[j], (tok_w, tok_w, tok_w, MEM_WIDTH), tm_s)
            mix_p = sb_prompt(q_p, kt_p, vt_p, sb_bias[j], 512)
            mix_s = sb_sample(q_s, _new_tokens_feature_major(k_s, n_s), _new_tokens_feature_major(v_s, n_s),
                              cache_kt, cache_vt, j, page_table, sb_bias[j], seq_s)
            sb_k_p.append(kt_p)
            sb_v_p.append(vt_p)
            sb_k_s.append(k_s.reshape(n_s, seq_s, sb_heads, HEAD_DIM))
            sb_v_s.append(v_s.reshape(n_s, seq_s, sb_heads, HEAD_DIM))
        mo_p = mem_attn(qm_p, mem_kt, mem_vt, i, n_p, tm_p)
        mo_s = mem_attn(qm_s, mem_kt_s, mem_vt_s, i, n_s, seq_s)
        xp = out_proj(xp, mix_p, mo_p, w_out_b[i], tm_p)
        xs = out_proj(xs, mix_s, mo_s, w_out_b[i], tm_s)
        last = i == depth - 1
        xp, y_p = mlp(xp, norm_mlp[i], w_up_b[i], w_down_b[i], norm_final, _pick_tile(tp, 1024), 1024, last)
        xs, y_s = mlp(xs, norm_mlp[i], w_up_b[i], w_down_b[i], norm_final, tm_s, 1024, last)

    def token_major(xt):
        n_l, n_seq, _, n_pos = xt.shape
        return jnp.transpose(xt.reshape(n_l, n_seq, -1, HEAD_DIM, n_pos), (0, 1, 4, 2, 3))

    return (y_p.reshape(n_p, seq_p, d_model), y_s.reshape(n_s, seq_s, d_model),
            jnp.stack(ssm_re_p), jnp.stack(ssm_im_p), jnp.stack(ssm_re_s), jnp.stack(ssm_im_s),
            token_major(jnp.stack(sb_k_p)), token_major(jnp.stack(sb_v_p)),
            jnp.stack(sb_k_s), jnp.stack(sb_v_s),
            token_major(mem_kt), token_major(mem_vt))
```

```python
import functools
import math

import jax
import jax.numpy as jnp
from jax import lax
from jax.experimental import pallas as pl
from jax.experimental.pallas import tpu as pltpu

F32 = jnp.float32
BF16 = jnp.bfloat16

RMS_EPS = 1e-6
HEAD_DIM = 64
ATTN_SCALE = 1.0 / math.sqrt(HEAD_DIM)
MEM_HEADS = 4
MEM_WIDTH = MEM_HEADS * HEAD_DIM
SSM_GROUP_CH = 16
SSM_STATE = 64
PAGE_SIZE = 128

LANES = 128
MXU_DIM = 256
SLAB_GROUPS = LANES // SSM_GROUP_CH
SLAB_STATE = SLAB_GROUPS * SSM_STATE
VMEM_LIMIT = 56 * 1024 * 1024


def _cparams(sem):
    return pltpu.CompilerParams(dimension_semantics=sem, vmem_limit_bytes=VMEM_LIMIT)


def _rmsnorm(x, g):
    ms = jnp.mean(x * x, axis=-1, keepdims=True)
    return x * lax.rsqrt(ms + RMS_EPS) * g


def _dot(a, b):
    return jnp.dot(a, b, preferred_element_type=F32)


def _dot_nt(a, b):
    return lax.dot_general(a, b, (((1,), (1,)), ((), ())), preferred_element_type=F32)


def _cmul(ar, ai, br, bi):
    return ar * br - ai * bi, ar * bi + ai * br


def _norm_proj_kernel(x_ref, g_ref, w_ref, *o_refs, widths):
    h = _rmsnorm(x_ref[...], g_ref[...]).astype(BF16)
    r = _dot(h, w_ref[...])
    off = 0
    for o_ref, wd in zip(o_refs, widths):
        o_ref[...] = r[:, off:off + wd]
        off += wd


def norm_proj(x, g, w_bf16, widths, tm):
    t, d = x.shape
    n = w_bf16.shape[1]
    assert sum(widths) == n and t % tm == 0
    return pl.pallas_call(
        functools.partial(_norm_proj_kernel, widths=widths),
        grid=(t // tm,),
        in_specs=[
            pl.BlockSpec((tm, d), lambda i: (i, 0)),
            pl.BlockSpec((1, d), lambda i: (0, 0)),
            pl.BlockSpec((d, n), lambda i: (0, 0)),
        ],
        out_specs=[pl.BlockSpec((tm, wd), lambda i: (i, 0)) for wd in widths],
        out_shape=[jax.ShapeDtypeStruct((t, wd), F32) for wd in widths],
        compiler_params=_cparams(("parallel",)),
        name="norm_proj",
    )(x, g.reshape(1, d), w_bf16)


def _sb_proj_kernel(x_ref, g_ref, wq_ref, wkvt_ref, q_ref, qm_ref, kt_ref, vt_ref, *, tok_w):
    h = _rmsnorm(x_ref[...], g_ref[...]).astype(BF16)
    r = _dot(h, wq_ref[...])
    q_ref[...] = r[:, :tok_w]
    qm_ref[...] = r[:, tok_w:]
    kv = _dot_nt(wkvt_ref[...], h)
    kt_ref[...] = kv[:tok_w]
    vt_ref[...] = kv[tok_w:]


def sb_proj(x, g, wq_bf16, wkvt_bf16, n_seq, tm):
    t, d = x.shape
    seq_len = t // n_seq
    nt = seq_len // tm
    tok_w = wkvt_bf16.shape[0] // 2
    nq = wq_bf16.shape[1]
    return pl.pallas_call(
        functools.partial(_sb_proj_kernel, tok_w=tok_w),
        grid=(n_seq, nt),
        in_specs=[
            pl.BlockSpec((tm, d), lambda b, i: (b * nt + i, 0)),
            pl.BlockSpec((1, d), lambda b, i: (0, 0)),
            pl.BlockSpec((d, nq), lambda b, i: (0, 0)),
            pl.BlockSpec((2 * tok_w, d), lambda b, i: (0, 0)),
        ],
        out_specs=[
            pl.BlockSpec((tm, tok_w), lambda b, i: (b * nt + i, 0)),
            pl.BlockSpec((tm, nq - tok_w), lambda b, i: (b * nt + i, 0)),
            pl.BlockSpec((None, tok_w, tm), lambda b, i: (b, 0, i)),
            pl.BlockSpec((None, tok_w, tm), lambda b, i: (b, 0, i)),
        ],
        out_shape=[
            jax.ShapeDtypeStruct((t, tok_w), F32),
            jax.ShapeDtypeStruct((t, nq - tok_w), F32),
            jax.ShapeDtypeStruct((n_seq, tok_w, seq_len), F32),
            jax.ShapeDtypeStruct((n_seq, tok_w, seq_len), F32),
        ],
        compiler_params=_cparams(("parallel", "parallel")),
        name="sb_proj",
    )(x, g.reshape(1, d), wq_bf16, wkvt_bf16)


def _mem_kv_kernel(x_ref, g_ref, wt_ref, k_ref, v_ref, h_scr):
    b = pl.program_id(1)

    @pl.when(pl.program_id(0) == 0)
    def _():
        h_scr[b] = _rmsnorm(x_ref[...], g_ref[...]).astype(BF16)

    r = _dot_nt(wt_ref[...], h_scr[b])
    k_ref[...] = r[:MEM_WIDTH]
    v_ref[...] = r[MEM_WIDTH:]


def mem_kv_all(mem, g, wt_bf16):
    n_seq, n_mem, d = mem.shape
    depth = wt_bf16.shape[0]
    return pl.pallas_call(
        _mem_kv_kernel,
        grid=(depth, n_seq),
        in_specs=[
            pl.BlockSpec((None, n_mem, d), lambda l, b: (b, 0, 0)),
            pl.BlockSpec((1, d), lambda l, b: (0, 0)),
            pl.BlockSpec((None, 2 * MEM_WIDTH, d), lambda l, b: (l, 0, 0)),
        ],
        out_specs=[pl.BlockSpec((None, None, MEM_WIDTH, n_mem), lambda l, b: (l, b, 0, 0))] * 2,
        out_shape=[jax.ShapeDtypeStruct((depth, n_seq, MEM_WIDTH, n_mem), F32)] * 2,
        scratch_shapes=[pltpu.VMEM((n_seq, n_mem, d), BF16)],
        compiler_params=_cparams(("arbitrary", "arbitrary")),
        name="mem_kv",
    )(mem, g.reshape(1, d), wt_bf16)


def _out_mlp_kernel(x_ref, mix_ref, mem_ref, w1_ref, w2_ref, g_ref, wu_ref, wd_ref, gf_ref, *refs,
                    nf, final_norm):
    if final_norm:
        o_ref, y_ref, x1_scr, h_scr, acc_scr = refs
    else:
        o_ref, x1_scr, h_scr, acc_scr = refs
    f = pl.program_id(1)

    @pl.when(f == 0)
    def _():
        proj = _dot(mix_ref[...].astype(BF16), w1_ref[...]) + _dot(mem_ref[...].astype(BF16), w2_ref[...])
        x1 = x_ref[...] + proj
        x1_scr[...] = x1
        h_scr[...] = _rmsnorm(x1, g_ref[...]).astype(BF16)
        acc_scr[...] = jnp.zeros_like(acc_scr)

    a = _dot(h_scr[...], wu_ref[...])
    a = jnp.square(jnp.maximum(a, 0.0)).astype(BF16)
    acc_scr[...] += _dot(a, wd_ref[...])

    @pl.when(f == nf - 1)
    def _():
        x2 = x1_scr[...] + acc_scr[...]
        o_ref[...] = x2
        if final_norm:
            y_ref[...] = _rmsnorm(x2, gf_ref[...])


def out_mlp(x, mix, mem_o, w_out_bf16, g, wu_bf16, wd_bf16, g_final, tm, tf, final_norm):
    t, d = x.shape
    n1 = mix.shape[1]
    n2 = mem_o.shape[1]
    ff = wu_bf16.shape[1]
    nf = ff // tf
    n_out = 2 if final_norm else 1
    outs = pl.pallas_call(
        functools.partial(_out_mlp_kernel, nf=nf, final_norm=final_norm),
        grid=(t // tm, nf),
        in_specs=[
            pl.BlockSpec((tm, d), lambda i, f: (i, 0)),
            pl.BlockSpec((tm, n1), lambda i, f: (i, 0)),
            pl.BlockSpec((tm, n2), lambda i, f: (i, 0)),
            pl.BlockSpec((n1, d), lambda i, f: (0, 0)),
            pl.BlockSpec((n2, d), lambda i, f: (0, 0)),
            pl.BlockSpec((1, d), lambda i, f: (0, 0)),
            pl.BlockSpec((d, tf), lambda i, f: (0, f)),
            pl.BlockSpec((tf, d), lambda i, f: (f, 0)),
            pl.BlockSpec((1, d), lambda i, f: (0, 0)),
        ],
        out_specs=[pl.BlockSpec((tm, d), lambda i, f: (i, 0))] * n_out,
        out_shape=[jax.ShapeDtypeStruct((t, d), F32)] * n_out,
        scratch_shapes=[pltpu.VMEM((tm, d), F32), pltpu.VMEM((tm, d), BF16), pltpu.VMEM((tm, d), F32)],
        compiler_params=_cparams(("parallel", "arbitrary")),
        name="out_mlp",
    )(x, mix, mem_o, w_out_bf16[:n1], w_out_bf16[n1:], g.reshape(1, d), wu_bf16, wd_bf16,
      g_final.reshape(1, d))
    return outs if final_norm else (outs[0], None)


def _glu_kernel(y_ref, w_ref, o_ref):
    g = jax.nn.gelu(y_ref[...])
    gm = _dot(g.astype(BF16), w_ref[...])
    o_ref[...] = g * jax.nn.sigmoid(gm)


def glu(y, w_bf16, tm):
    t, n = y.shape
    return pl.pallas_call(
        _glu_kernel,
        grid=(t // tm,),
        in_specs=[pl.BlockSpec((tm, n), lambda i: (i, 0)), pl.BlockSpec((n, n), lambda i: (0, 0))],
        out_specs=pl.BlockSpec((tm, n), lambda i: (i, 0)),
        out_shape=jax.ShapeDtypeStruct((t, n), F32),
        compiler_params=_cparams(("parallel",)),
        name="glu",
    )(y, w_bf16)


def _mem_attn_kernel(q_ref, kt_ref, vt_ref, o_ref):
    q = q_ref[...] * ATTN_SCALE
    kt = kt_ref[...].astype(BF16)
    vt = vt_ref[...].astype(BF16)
    lane_head = lax.broadcasted_iota(jnp.int32, (1, MEM_WIDTH), 1) // HEAD_DIM
    out = jnp.zeros(q.shape, F32)
    for h in range(MEM_HEADS):
        m = lane_head == h
        s = _dot(jnp.where(m, q, 0.0).astype(BF16), kt)
        p = jnp.exp(s - jnp.max(s, axis=-1, keepdims=True))
        p = p / jnp.sum(p, axis=-1, keepdims=True)
        out = out + jnp.where(m, _dot_nt(p.astype(BF16), vt), 0.0)
    o_ref[...] = out


def mem_attn(qm, mkt, mvt, layer, n_seq, tm):
    t = qm.shape[0]
    seq_len = t // n_seq
    n_mem = mkt.shape[-1]
    nt = seq_len // tm
    kv_spec = pl.BlockSpec((None, None, MEM_WIDTH, n_mem), lambda b, i: (layer, b, 0, 0))
    return pl.pallas_call(
        _mem_attn_kernel,
        grid=(n_seq, nt),
        in_specs=[pl.BlockSpec((tm, MEM_WIDTH), lambda b, i: (b * nt + i, 0)), kv_spec, kv_spec],
        out_specs=pl.BlockSpec((tm, MEM_WIDTH), lambda b, i: (b * nt + i, 0)),
        out_shape=jax.ShapeDtypeStruct((t, MEM_WIDTH), F32),
        compiler_params=_cparams(("parallel", "parallel")),
        name="mem_attn",
    )(qm, mkt, mvt)


def _mem_attn_short_kernel(q_ref, kt_ref, vt_ref, o_ref, *, n_q):
    n_b = kt_ref.shape[0]
    rows = MEM_HEADS * n_q
    row = lax.broadcasted_iota(jnp.int32, (rows, 1), 0)
    head_mask = (lax.broadcasted_iota(jnp.int32, (1, MEM_WIDTH), 1) // HEAD_DIM) == (row // n_q)
    for b in range(n_b):
        q = q_ref[b * n_q:(b + 1) * n_q, :] * ATTN_SCALE
        qbd = jnp.where(head_mask, jnp.concatenate([q] * MEM_HEADS, axis=0), 0.0).astype(BF16)
        s = _dot(qbd, kt_ref[b].astype(BF16))
        p = jnp.exp(s - jnp.max(s, axis=-1, keepdims=True))
        p = p / jnp.sum(p, axis=-1, keepdims=True)
        a = jnp.where(head_mask, _dot_nt(p.astype(BF16), vt_ref[b].astype(BF16)), 0.0)
        out = a[0:n_q]
        for h in range(1, MEM_HEADS):
            out = out + a[h * n_q:(h + 1) * n_q]
        o_ref[b * n_q:(b + 1) * n_q, :] = out


def mem_attn_short(qm, mkt, mvt, layer, n_seq, n_b):
    t = qm.shape[0]
    n_q = t // n_seq
    n_mem = mkt.shape[-1]
    kv_spec = pl.BlockSpec((None, n_b, MEM_WIDTH, n_mem), lambda i: (layer, i, 0, 0))
    return pl.pallas_call(
        functools.partial(_mem_attn_short_kernel, n_q=n_q),
        grid=(n_seq // n_b,),
        in_specs=[pl.BlockSpec((n_b * n_q, MEM_WIDTH), lambda i: (i, 0)), kv_spec, kv_spec],
        out_specs=pl.BlockSpec((n_b * n_q, MEM_WIDTH), lambda i: (i, 0)),
        out_shape=jax.ShapeDtypeStruct((t, MEM_WIDTH), F32),
        compiler_params=_cparams(("parallel",)),
        name="mem_attn_short",
    )(qm, mkt, mvt)


def _ssm_prep_kernel(lr_ref, li_ref, ls_ref, lrc_ref, lic_ref, lsc_ref, btr_ref, bti_ref, ctr_ref, cti_ref,
                     m_ref, bc_ref, cc_ref, apr_ref, api_ref, hpr_ref, hpi_ref,
                     hc_scr, d_scr, pr_scr, pi_scr, bbr_scr, bbi_scr, *, lc, n_pow):
    s = pl.program_id(1)

    def discretise(lam_re, lam_im, log_step):
        a_re = jnp.minimum(lam_re, -1e-4)
        step = jnp.exp(log_step)
        mag = jnp.exp(step * a_re)
        return a_re, lam_im, mag * jnp.cos(step * lam_im), mag * jnp.sin(step * lam_im)

    @pl.when(s == 0)
    def _():
        a_re, a_im, lbr, lbi = discretise(lr_ref[...], li_ref[...], ls_ref[...])
        den = a_re * a_re + a_im * a_im
        nr, ni = _cmul(lbr - 1.0, lbi, a_re, -a_im)
        bbr, bbi = _cmul(nr / den, ni / den, btr_ref[...], bti_ref[...])
        bbr_scr[...] = bbr
        bbi_scr[...] = bbi
        pr, pi = jnp.ones_like(lbr), jnp.zeros_like(lbi)
        for j in range(lc + 1):
            pr_scr[j] = pr
            pi_scr[j] = pi
            if j == lc // 2:
                hpr_ref[...] = pr
                hpi_ref[...] = pi
            if j == lc:
                ar, ai = pr, pi
                for k in range(n_pow):
                    apr_ref[k:k + 1, :] = ar
                    api_ref[k:k + 1, :] = ai
                    ar, ai = _cmul(ar, ai, ar, ai)
            pr, pi = _cmul(pr, pi, lbr, lbi)
        _, _, lbrc, lbic = discretise(lrc_ref[...], lic_ref[...], lsc_ref[...])
        shape = ctr_ref.shape
        lbrc = jnp.broadcast_to(lbrc, shape)
        lbic = jnp.broadcast_to(lbic, shape)
        qr, qi = ctr_ref[...], cti_ref[...]
        bcat = jnp.concatenate([bbr, bbi], axis=1)
        for j in range(lc + 1):
            hj = jnp.concatenate([qr, -qi], axis=0)
            hc_scr[j] = hj
            if j < lc:
                d_scr[j] = jnp.dot(bcat, hj, precision=lax.Precision.HIGHEST, preferred_element_type=F32)
            qr, qi = _cmul(qr, qi, lbrc, lbic)

    pr, pi = pr_scr[s], pi_scr[s]
    gr, gi = _cmul(bbr_scr[...], bbi_scr[...], pr, pi)
    bc_ref[...] = jnp.concatenate([gr, gi], axis=1).astype(BF16)
    cc_ref[...] = hc_scr[s + 1].astype(BF16)
    for t in range(lc):
        blk = d_scr[jnp.maximum(t - s, 0)]
        m_ref[:, t * LANES:(t + 1) * LANES] = jnp.where(t >= s, blk, 0.0).astype(BF16)


def ssm_operators(lam_re, lam_im, log_step, b_re, b_im, c_re, c_im, lc, n_pow):
    n_g, n_p, n_c = b_re.shape
    n_slab = n_g // SLAB_GROUPS
    same = jnp.eye(SLAB_GROUPS, dtype=bool)

    def rows(x):
        return x.astype(F32).reshape(n_slab, 1, SLAB_STATE)

    def cols(x):
        return x.astype(F32).reshape(n_slab, SLAB_STATE, 1)

    def bt_blocks(b):
        bt = jnp.transpose(b.astype(F32), (0, 2, 1)).reshape(n_slab, SLAB_GROUPS, n_c, 1, n_p)
        bt = jnp.where(same[None, :, None, :, None], bt, 0.0)
        return bt.reshape(n_slab, LANES, SLAB_STATE)

    def ct_blocks(c):
        ct = jnp.transpose(c.astype(F32), (0, 2, 1)).reshape(n_slab, SLAB_GROUPS, n_p, 1, n_c)
        ct = jnp.where(same[None, :, None, :, None], ct, 0.0)
        return ct.reshape(n_slab, SLAB_STATE, LANES)

    step_gp = jnp.broadcast_to(log_step.astype(F32)[:, None], (n_g, n_p))
    row_spec = pl.BlockSpec((None, 1, SLAB_STATE), lambda g, s: (g, 0, 0))
    col_spec = pl.BlockSpec((None, SLAB_STATE, 1), lambda g, s: (g, 0, 0))
    bt_spec = pl.BlockSpec((None, LANES, SLAB_STATE), lambda g, s: (g, 0, 0))
    ct_spec = pl.BlockSpec((None, SLAB_STATE, LANES), lambda g, s: (g, 0, 0))
    pow_spec = pl.BlockSpec((None, n_pow, SLAB_STATE), lambda g, s: (g, 0, 0))
    kdim = lc * LANES
    half_shape = jax.ShapeDtypeStruct((n_slab, 1, SLAB_STATE), F32)
    return pl.pallas_call(
        functools.partial(_ssm_prep_kernel, lc=lc, n_pow=n_pow),
        grid=(n_slab, lc),
        in_specs=[row_spec] * 3 + [col_spec] * 3 + [bt_spec] * 2 + [ct_spec] * 2,
        out_specs=[
            pl.BlockSpec((None, LANES, kdim), lambda g, s: (g, s, 0)),
            pl.BlockSpec((None, LANES, 2 * SLAB_STATE), lambda g, s: (g, lc - 1 - s, 0)),
            pl.BlockSpec((None, 2 * SLAB_STATE, LANES), lambda g, s: (g, 0, s)),
            pow_spec, pow_spec, row_spec, row_spec,
        ],
        out_shape=[
            jax.ShapeDtypeStruct((n_slab, kdim, kdim), BF16),
            jax.ShapeDtypeStruct((n_slab, kdim, 2 * SLAB_STATE), BF16),
            jax.ShapeDtypeStruct((n_slab, 2 * SLAB_STATE, kdim), BF16),
            jax.ShapeDtypeStruct((n_slab, n_pow, SLAB_STATE), F32),
            jax.ShapeDtypeStruct((n_slab, n_pow, SLAB_STATE), F32),
            half_shape, half_shape,
        ],
        scratch_shapes=[
            pltpu.VMEM((lc + 1, 2 * SLAB_STATE, LANES), F32),
            pltpu.VMEM((lc, LANES, LANES), F32),
            pltpu.VMEM((lc + 1, 1, SLAB_STATE), F32),
            pltpu.VMEM((lc + 1, 1, SLAB_STATE), F32),
            pltpu.VMEM((LANES, SLAB_STATE), F32),
            pltpu.VMEM((LANES, SLAB_STATE), F32),
        ],
        compiler_params=_cparams(("parallel", "arbitrary")),
        name="ssm_prep",
    )(rows(lam_re), rows(lam_im), rows(step_gp), cols(lam_re), cols(lam_im), cols(step_gp),
      bt_blocks(b_re), bt_blocks(b_im), ct_blocks(c_re), ct_blocks(c_im))


def _ssm_kernel(u_ref, x0r_ref, x0i_ref, m_ref, bc_ref, cc_ref, d_ref, apr_ref, api_ref,
                y_ref, xr_ref, xi_ref, *, lc, nc):
    rows = u_ref.shape[0] // lc
    nb = rows // nc
    us = [u_ref[pl.ds(s, rows, stride=lc), :] for s in range(lc)]
    ucat = jnp.concatenate([u.astype(BF16) for u in us], axis=1)
    grp = MXU_DIM // LANES
    y = jnp.concatenate(
        [_dot(ucat[:, :(j + 1) * MXU_DIM], m_ref[:(j + 1) * MXU_DIM, j * MXU_DIM:(j + 1) * MXU_DIM])
         for j in range(lc // grp)], axis=1)
    v = _dot(ucat, bc_ref[...])
    vr, vi = v[:, :SLAB_STATE], v[:, SLAB_STATE:]
    x0r, x0i = x0r_ref[...], x0i_ref[...]
    inj_r, inj_i = _cmul(apr_ref[0:1, :], api_ref[0:1, :], x0r, x0i)
    if nc == 1:
        xr, xi = vr + inj_r, vi + inj_i
        pr, pi = x0r, x0i
        xr_ref[...] = xr
        xi_ref[...] = xi
    else:
        row = lax.broadcasted_iota(jnp.int32, (rows, 1), 0)
        chunk = row % nc

        def first_rows(x):
            out = jnp.zeros((rows, x.shape[1]), F32)
            for b in range(nb):
                out = jnp.where(row == b * nc, x[b:b + 1, :], out)
            return out

        vr = vr + first_rows(inj_r)
        vi = vi + first_rows(inj_i)
        for k in range(nc.bit_length() - 1):
            sh = 1 << k
            sr = jnp.where(chunk >= sh, pltpu.roll(vr, sh, 0), 0.0)
            si = jnp.where(chunk >= sh, pltpu.roll(vi, sh, 0), 0.0)
            dr, di = _cmul(apr_ref[k:k + 1, :], api_ref[k:k + 1, :], sr, si)
            vr, vi = vr + dr, vi + di
        for b in range(nb):
            last = (b + 1) * nc - 1
            xr_ref[b:b + 1, :] = vr[last:last + 1, :]
            xi_ref[b:b + 1, :] = vi[last:last + 1, :]
        pr = jnp.where(chunk >= 1, pltpu.roll(vr, 1, 0), first_rows(x0r))
        pi = jnp.where(chunk >= 1, pltpu.roll(vi, 1, 0), first_rows(x0i))
    prev = jnp.concatenate([pr.astype(BF16), pi.astype(BF16)], axis=1)
    y = y + _dot(prev, cc_ref[...])
    d = d_ref[...]
    for s in range(lc):
        y_ref[pl.ds(s, rows, stride=lc), :] = y[:, s * LANES:(s + 1) * LANES] + d * us[s]


def ssm_apply(u, x0r, x0i, ops, d_skip, lc, nc, nb):
    m, bc, cc, apr, api = ops
    bc_blk = bc.shape[1] // (lc * LANES) - 1
    t, width = u.shape
    n_slab = width // LANES
    tok = nb * nc * lc
    n_blk = t // tok
    kdim = lc * LANES
    n_pow = apr.shape[1]
    state_spec = pl.BlockSpec((None, None, nb, SLAB_STATE), lambda s, b: (s, b, 0, 0))
    pow_spec = pl.BlockSpec((None, n_pow, SLAB_STATE), lambda s, b: (s, 0, 0))
    state_shape = jax.ShapeDtypeStruct((n_slab, n_blk, nb, SLAB_STATE), F32)
    return pl.pallas_call(
        functools.partial(_ssm_kernel, lc=lc, nc=nc),
        grid=(n_slab, n_blk),
        in_specs=[
            pl.BlockSpec((tok, LANES), lambda s, b: (b, s)),
            state_spec, state_spec,
            pl.BlockSpec((None, kdim, kdim), lambda s, b: (s, 0, 0)),
            pl.BlockSpec((None, kdim, 2 * SLAB_STATE), lambda s, b: (s, bc_blk, 0)),
            pl.BlockSpec((None, 2 * SLAB_STATE, kdim), lambda s, b: (s, 0, 0)),
            pl.BlockSpec((None, 1, LANES), lambda s, b: (s, 0, 0)),
            pow_spec, pow_spec,
        ],
        out_specs=[pl.BlockSpec((tok, LANES), lambda s, b: (b, s)), state_spec, state_spec],
        out_shape=[jax.ShapeDtypeStruct((t, width), F32), state_shape, state_shape],
        compiler_params=_cparams(("arbitrary", "arbitrary")),
        name="ssm",
    )(u, x0r, x0i, m, bc, cc, d_skip.astype(F32).reshape(n_slab, 1, LANES), apr, api)


def _state_to_slabs(x, nb):
    n_seq, n_g, n_p = x.shape
    n_slab = n_g // SLAB_GROUPS
    x = x.reshape(n_seq, n_slab, SLAB_STATE)
    return jnp.transpose(x, (1, 0, 2)).reshape(n_slab, n_seq // nb, nb, SLAB_STATE)


def _slabs_to_state(x, n_p):
    n_slab, n_blk, nb, _ = x.shape
    x = jnp.transpose(x.reshape(n_slab, n_blk * nb, SLAB_STATE), (1, 0, 2))
    return x.reshape(n_blk * nb, n_slab * SLAB_GROUPS, n_p)


def _sb_weights(qh, kt_blk, bias, tri, carry, mask):
    n_keys = kt_blk.shape[1]
    n_tri = tri.shape[0]
    z = _dot(qh, kt_blk) + bias
    sp = jnp.maximum(z, 0.0) + jnp.log(1.0 + jnp.exp(-jnp.abs(z)))
    if mask is not None:
        sp = jnp.where(mask, sp, 0.0)
    spb = sp.astype(BF16)
    parts = []
    for c in reversed(range(n_keys // n_tri)):
        sl = slice(c * n_tri, (c + 1) * n_tri)
        parts.insert(0, _dot(spb[:, sl], tri) + carry)
        carry = carry + jnp.sum(sp[:, sl], axis=-1, keepdims=True)
    suffix = parts[0] if len(parts) == 1 else jnp.concatenate(parts, axis=1)
    w = jnp.exp(z - sp - suffix)
    if mask is not None:
        w = jnp.where(mask, w, 0.0)
    return w.astype(BF16), carry


def _sb_prompt_kernel(bias_ref, q_ref, kt_ref, vt_ref, tri_ref, o_ref, *, tq):
    pair = pl.program_id(1)
    i = pl.program_id(2)
    n_h = LANES // HEAD_DIM
    q = q_ref[...] * ATTN_SCALE
    tri = tri_ref[...]
    lane_head = lax.broadcasted_iota(jnp.int32, (1, LANES), 1) // HEAD_DIM
    qpos = lax.broadcasted_iota(jnp.int32, (tq, tq), 0)
    kpos = lax.broadcasted_iota(jnp.int32, (tq, tq), 1)
    diag_mask = kpos < qpos
    qhs = [jnp.where(lane_head == hh, q, 0.0).astype(BF16) for hh in range(n_h)]
    biases = [bias_ref[pair * n_h + hh] for hh in range(n_h)]

    def blocks(start, carries, mask):
        kt = kt_ref[:, pl.ds(start, tq)].astype(BF16)
        vt = vt_ref[:, pl.ds(start, tq)].astype(BF16)
        res = []
        for hh in range(n_h):
            w, c = _sb_weights(qhs[hh], kt, biases[hh], tri, carries[hh], mask)
            res.append((_dot_nt(w, vt), c))
        return res

    zero = jnp.zeros((tq, 1), F32)
    res = blocks(pl.multiple_of(i * tq, tq), [zero] * n_h, diag_mask)

    def body(it, carry):
        res = blocks(pl.multiple_of((i - 1 - it) * tq, tq), [c for _, c in carry], None)
        return tuple((o + ob, c) for (o, _), (ob, c) in zip(carry, res))

    res = lax.fori_loop(0, i, body, tuple(res))
    out = res[0][0]
    for hh in range(1, n_h):
        out = jnp.where(lane_head == hh, res[hh][0], out)
    o_ref[...] = out


def _tri(n):
    r = jnp.arange(n)
    return (r[:, None] > r[None, :]).astype(BF16)


def sb_prompt(q, kt, vt, bias, tq):
    t, width = q.shape
    n_seq, _, seq_len = kt.shape
    nq = seq_len // tq
    n_pair = width // LANES
    kv_spec = pl.BlockSpec((None, LANES, seq_len), lambda b, p, i: (b, p, 0))
    return pl.pallas_call(
        functools.partial(_sb_prompt_kernel, tq=tq),
        grid=(n_seq, n_pair, nq),
        in_specs=[
            pl.BlockSpec(memory_space=pltpu.SMEM),
            pl.BlockSpec((tq, LANES), lambda b, p, i: (b * nq + i, p)),
            kv_spec, kv_spec,
            pl.BlockSpec((MXU_DIM, MXU_DIM), lambda b, p, i: (0, 0)),
        ],
        out_specs=pl.BlockSpec((tq, LANES), lambda b, p, i: (b * nq + i, p)),
        out_shape=jax.ShapeDtypeStruct((t, width), F32),
        compiler_params=_cparams(("parallel", "parallel", "arbitrary")),
        name="sb_prompt",
    )(bias.astype(F32), q, kt, vt, _tri(MXU_DIM))


def _sb_sample_kernel(pt_ref, q_ref, knt_ref, vnt_ref, bias_ref, tri_ref, *refs, n_q, n_pg):
    k_refs = refs[:n_pg]
    v_refs = refs[n_pg:2 * n_pg]
    o_ref = refs[2 * n_pg]
    acc_scr, carry_scr = refs[2 * n_pg + 1:]
    i = pl.program_id(1)
    width = q_ref.shape[1]
    n_heads = width // HEAD_DIM
    rows = bias_ref.shape[0]
    row = lax.broadcasted_iota(jnp.int32, (rows, 1), 0)
    head_mask = (lax.broadcasted_iota(jnp.int32, (1, width), 1) // HEAD_DIM) == (row // n_q)
    q = q_ref[...] * ATTN_SCALE
    qrep = jnp.concatenate([q] * (rows // n_q), axis=0)
    qbd = jnp.where(head_mask, qrep, 0.0).astype(BF16)
    bias = bias_ref[...]
    tri = tri_ref[...]

    @pl.when(i == 0)
    def _():
        kpos = lax.broadcasted_iota(jnp.int32, (rows, PAGE_SIZE), 1)
        mask = kpos < (row % n_q)
        w, c = _sb_weights(qbd, knt_ref[...].astype(BF16), bias, tri[:PAGE_SIZE, :PAGE_SIZE],
                           jnp.zeros((rows, 1), F32), mask)
        acc_scr[...] = _dot_nt(vnt_ref[...].astype(BF16), w)
        carry_scr[...] = c

    acc = acc_scr[...]
    carry = carry_scr[...]
    per_blk = MXU_DIM // PAGE_SIZE
    for j in range(n_pg // per_blk):
        pages = [j * per_blk + r for r in reversed(range(per_blk))]
        kt = jnp.concatenate([k_refs[p][...] for p in pages], axis=1).astype(BF16)
        vt = jnp.concatenate([v_refs[p][...] for p in pages], axis=1).astype(BF16)
        w, carry = _sb_weights(qbd, kt, bias, tri, carry, None)
        acc = acc + _dot_nt(vt, w)
    acc_scr[...] = acc
    carry_scr[...] = carry

    @pl.when(i == pl.num_programs(1) - 1)
    def _():
        a = jnp.where(head_mask, jnp.transpose(acc), 0.0)
        out = a[0:n_q]
        for h in range(1, n_heads):
            out = out + a[h * n_q:(h + 1) * n_q]
        o_ref[...] = out


def sb_sample(q, knt, vnt, cache_kt, cache_vt, layer, page_table, bias, n_q, n_pg=16):
    t, width = q.shape
    n_seq = t // n_q
    n_pages = page_table.shape[1]
    rows = (width // HEAD_DIM) * n_q
    rows_pad = -(-rows // LANES) * LANES
    bias_col = jnp.pad(jnp.repeat(bias.astype(F32), n_q), (0, rows_pad - rows)).reshape(rows_pad, 1)

    def page_spec(r):
        return pl.BlockSpec((None, None, width, PAGE_SIZE),
                            lambda b, i, pt: (layer, pt[b, n_pages - 1 - (i * n_pg + r)], 0, 0))

    new_spec = pl.BlockSpec((None, width, PAGE_SIZE), lambda b, i, pt: (b, 0, 0))
    grid_spec = pltpu.PrefetchScalarGridSpec(
        num_scalar_prefetch=1,
        grid=(n_seq, n_pages // n_pg),
        in_specs=[
            pl.BlockSpec((n_q, width), lambda b, i, pt: (b, 0)),
            new_spec, new_spec,
            pl.BlockSpec((rows_pad, 1), lambda b, i, pt: (0, 0)),
            pl.BlockSpec((MXU_DIM, MXU_DIM), lambda b, i, pt: (0, 0)),
        ] + [page_spec(r) for r in range(n_pg)] * 2,
        out_specs=pl.BlockSpec((n_q, width), lambda b, i, pt: (b, 0)),
        scratch_shapes=[pltpu.VMEM((width, rows_pad), F32), pltpu.VMEM((rows_pad, 1), F32)],
    )
    return pl.pallas_call(
        functools.partial(_sb_sample_kernel, n_q=n_q, n_pg=n_pg),
        grid_spec=grid_spec,
        out_shape=jax.ShapeDtypeStruct((t, width), F32),
        compiler_params=_cparams(("parallel", "arbitrary")),
        name="sb_sample",
    )(page_table, q, knt, vnt, bias_col, _tri(MXU_DIM), *([cache_kt] * n_pg), *([cache_vt] * n_pg))


def _pick_tile(t, pref):
    while t % pref:
        pref //= 2
    return pref


def _feature_major_pages(cache):
    n_l, pool, page, heads, d = cache.shape
    return jnp.transpose(cache, (0, 1, 3, 4, 2)).reshape(n_l, pool, heads * d, page)


def _new_tokens_feature_major(x, n_seq):
    n_q = x.shape[0] // n_seq
    xt = jnp.transpose(x.reshape(n_seq, n_q, x.shape[1]), (0, 2, 1))
    return jnp.pad(xt, ((0, 0), (0, 0), (0, PAGE_SIZE - n_q)))


def kernel(x_prompt, x_sample, mem_prompt, cache_sb_k, cache_sb_v, page_table, cache_mem_k, cache_mem_v, state_ssm_re, state_ssm_im, norm_mix, norm_mlp, norm_mem, norm_final, w_in_ssm, ssm_lambda_re, ssm_lambda_im, ssm_log_step, ssm_b_re, ssm_b_im, ssm_c_re, ssm_c_im, ssm_d, ssm_w_glu, w_in_sb, sb_bias, w_mem_kv, w_out, w_up, w_down):
    n_p, seq_p, d_model = x_prompt.shape
    n_s, seq_s, _ = x_sample.shape
    depth = w_out.shape[0]
    n_mem = mem_prompt.shape[1]
    tok_w = ssm_d.shape[1]
    n_groups, n_state = ssm_lambda_re.shape[1:]
    sb_heads = tok_w // HEAD_DIM
    lc_p = 16
    nc_p = seq_p // lc_p

    xp = x_prompt.reshape(n_p * seq_p, d_model)
    xs = x_sample.reshape(n_s * seq_s, d_model)
    tp, ts = xp.shape[0], xs.shape[0]
    tm_p = _pick_tile(seq_p, 512)
    tm_s = _pick_tile(ts, 256)

    w_in_ssm_b = w_in_ssm.astype(BF16)
    w_in_sb_b = w_in_sb.astype(BF16)
    wq_sb_b = jnp.concatenate([w_in_sb_b[:, :, :tok_w], w_in_sb_b[:, :, 3 * tok_w:]], axis=2)
    wkvt_sb_b = jnp.transpose(w_in_sb_b[:, :, tok_w:3 * tok_w], (0, 2, 1))
    w_out_b = w_out.astype(BF16)
    w_up_b = w_up.astype(BF16)
    w_down_b = w_down.astype(BF16)
    w_glu_b = ssm_w_glu.astype(BF16)
    cache_kt = _feature_major_pages(cache_sb_k)
    cache_vt = _feature_major_pages(cache_sb_v)
    n_lm, _, _, mem_heads, _ = cache_mem_k.shape
    mem_kt_s = jnp.transpose(cache_mem_k, (0, 1, 3, 4, 2)).reshape(n_lm, n_s, MEM_WIDTH, n_mem)
    mem_vt_s = jnp.transpose(cache_mem_v, (0, 1, 3, 4, 2)).reshape(n_lm, n_s, MEM_WIDTH, n_mem)

    mem_kt, mem_vt = mem_kv_all(mem_prompt, norm_mem, jnp.transpose(w_mem_kv, (0, 2, 1)).astype(BF16))

    ssm_re_p, ssm_im_p, ssm_re_s, ssm_im_s = [], [], [], []
    sb_k_p, sb_v_p, sb_k_s, sb_v_s = [], [], [], []
    y_p = y_s = None
    for i in range(depth):
        j = i // 2
        if i % 2 == 0:
            prm = (ssm_lambda_re[j], ssm_lambda_im[j], ssm_log_step[j], ssm_b_re[j], ssm_b_im[j],
                   ssm_c_re[j], ssm_c_im[j])
            m_op, bc_op, cc_op, apr, api, hpr, hpi = ssm_operators(*prm, lc=lc_p, n_pow=nc_p.bit_length() - 1)
            ops_p = (m_op, bc_op, cc_op, apr, api)
            if 2 * seq_s == lc_p:
                ops_s = (m_op, bc_op, cc_op, hpr, hpi)
            else:
                ops_s = ssm_operators(*prm, lc=seq_s, n_pow=1)[:5]
            u_p, qm_p = norm_proj(xp, norm_mix[i], w_in_ssm_b[j], (tok_w, MEM_WIDTH), tm_p)
            u_s, qm_s = norm_proj(xs, norm_mix[i], w_in_ssm_b[j], (tok_w, MEM_WIDTH), tm_s)
            zero = jnp.zeros((n_p, n_groups, n_state), F32)
            nb_p = 2 if n_p % 2 == 0 else 1
            yv_p, fr_p, fi_p = ssm_apply(u_p, _state_to_slabs(zero, nb_p), _state_to_slabs(zero, nb_p),
                                         ops_p, ssm_d[j], lc_p, nc_p, nb_p)
            yv_s, fr_s, fi_s = ssm_apply(u_s, _state_to_slabs(state_ssm_re[j].astype(F32), n_s),
                                         _state_to_slabs(state_ssm_im[j].astype(F32), n_s),
                                         ops_s, ssm_d[j], seq_s, 1, n_s)
            mix_p = glu(yv_p, w_glu_b[j], tm_p)
            mix_s = glu(yv_s, w_glu_b[j], tm_s)
            ssm_re_p.append(_slabs_to_state(fr_p, n_state))
            ssm_im_p.append(_slabs_to_state(fi_p, n_state))
            ssm_re_s.append(_slabs_to_state(fr_s, n_state))
            ssm_im_s.append(_slabs_to_state(fi_s, n_state))
        else:
            q_p, qm_p, kt_p, vt_p = sb_proj(xp, norm_mix[i], wq_sb_b[j], wkvt_sb_b[j], n_p, tm_p)
            q_s, k_s, v_s, qm_s = norm_proj(xs, norm_mix[i], w_in_sb_b[j], (tok_w, tok_w, tok_w, MEM_WIDTH), tm_s)
            mix_p = sb_prompt(q_p, kt_p, vt_p, sb_bias[j], 512)
            mix_s = sb_sample(q_s, _new_tokens_feature_major(k_s, n_s), _new_tokens_feature_major(v_s, n_s),
                              cache_kt, cache_vt, j, page_table, sb_bias[j], seq_s)
            sb_k_p.append(kt_p)
            sb_v_p.append(vt_p)
            sb_k_s.append(k_s.reshape(n_s, seq_s, sb_heads, HEAD_DIM))
            sb_v_s.append(v_s.reshape(n_s, seq_s, sb_heads, HEAD_DIM))
        mo_p = mem_attn(qm_p, mem_kt, mem_vt, i, n_p, tm_p)
        mo_s = mem_attn_short(qm_s, mem_kt_s, mem_vt_s, i, n_s, _pick_tile(n_s, 8))
        last = i == depth - 1
        xp, y_p = out_mlp(xp, mix_p, mo_p, w_out_b[i], norm_mlp[i], w_up_b[i], w_down_b[i], norm_final,
                          tm_p, 1024, last)
        xs, y_s = out_mlp(xs, mix_s, mo_s, w_out_b[i], norm_mlp[i], w_up_b[i], w_down_b[i], norm_final,
                          tm_s, 1024, last)

    def token_major(xt):
        n_l, n_seq, _, n_pos = xt.shape
        return jnp.transpose(xt.reshape(n_l, n_seq, -1, HEAD_DIM, n_pos), (0, 1, 4, 2, 3))

    return (y_p.reshape(n_p, seq_p, d_model), y_s.reshape(n_s, seq_s, d_model),
            jnp.stack(ssm_re_p), jnp.stack(ssm_im_p), jnp.stack(ssm_re_s), jnp.stack(ssm_im_s),
            token_major(jnp.stack(sb_k_p)), token_major(jnp.stack(sb_v_p)),
            jnp.stack(sb_k_s), jnp.stack(sb_v_s),
            token_major(mem_kt), token_major(mem_vt))
```

```python
import functools
import math

import jax
import jax.numpy as jnp
from jax import lax
from jax.experimental import pallas as pl
from jax.experimental.pallas import tpu as pltpu

F32 = jnp.float32
BF16 = jnp.bfloat16

RMS_EPS = 1e-6
HEAD_DIM = 64
ATTN_SCALE = 1.0 / math.sqrt(HEAD_DIM)
LOG2_E = 1.0 / math.log(2.0)
MEM_HEADS = 4
MEM_WIDTH = MEM_HEADS * HEAD_DIM
SSM_GROUP_CH = 16
SSM_STATE = 64
PAGE_SIZE = 128

LANES = 128
MXU_DIM = 256
SLAB_GROUPS = LANES // SSM_GROUP_CH
SLAB_STATE = SLAB_GROUPS * SSM_STATE
VMEM_LIMIT = 56 * 1024 * 1024


def _cparams(sem):
    return pltpu.CompilerParams(dimension_semantics=sem, vmem_limit_bytes=VMEM_LIMIT)


def _rmsnorm(x, g):
    ms = jnp.mean(x * x, axis=-1, keepdims=True)
    return x * lax.rsqrt(ms + RMS_EPS) * g


def _dot(a, b):
    return jnp.dot(a, b, preferred_element_type=F32)


def _dot_nt(a, b):
    return lax.dot_general(a, b, (((1,), (1,)), ((), ())), preferred_element_type=F32)


def _cmul(ar, ai, br, bi):
    return ar * br - ai * bi, ar * bi + ai * br


def _norm_proj_kernel(x_ref, g_ref, w_ref, *o_refs, widths):
    h = _rmsnorm(x_ref[...], g_ref[...]).astype(BF16)
    r = _dot(h, w_ref[...])
    off = 0
    for o_ref, wd in zip(o_refs, widths):
        o_ref[...] = r[:, off:off + wd]
        off += wd


def norm_proj(x, g, w_bf16, widths, tm):
    t, d = x.shape
    n = w_bf16.shape[1]
    assert sum(widths) == n and t % tm == 0
    return pl.pallas_call(
        functools.partial(_norm_proj_kernel, widths=widths),
        grid=(t // tm,),
        in_specs=[
            pl.BlockSpec((tm, d), lambda i: (i, 0)),
            pl.BlockSpec((1, d), lambda i: (0, 0)),
            pl.BlockSpec((d, n), lambda i: (0, 0)),
        ],
        out_specs=[pl.BlockSpec((tm, wd), lambda i: (i, 0)) for wd in widths],
        out_shape=[jax.ShapeDtypeStruct((t, wd), F32) for wd in widths],
        compiler_params=_cparams(("parallel",)),
        name="norm_proj",
    )(x, g.reshape(1, d), w_bf16)


def _sb_proj_kernel(x_ref, g_ref, wq_ref, wkvt_ref, *refs, tok_w, n_prev):
    if n_prev:
        kt_prev_ref, vt_prev_ref, q_ref, qm_ref, kt_ref, vt_ref = refs
        kt_ref[:n_prev] = kt_prev_ref[...]
        vt_ref[:n_prev] = vt_prev_ref[...]
    else:
        q_ref, qm_ref, kt_ref, vt_ref = refs
    h = _rmsnorm(x_ref[...], g_ref[...]).astype(BF16)
    r = _dot(h, wq_ref[...])
    q_ref[...] = r[:, :tok_w]
    qm_ref[...] = r[:, tok_w:]
    kv = _dot_nt(wkvt_ref[...], h)
    kt_ref[n_prev] = kv[:tok_w]
    vt_ref[n_prev] = kv[tok_w:]


def sb_proj(x, g, wq_bf16, wkvt_bf16, n_seq, tm, kt_prev=None, vt_prev=None):
    t, d = x.shape
    seq_len = t // n_seq
    nt = seq_len // tm
    tok_w = wkvt_bf16.shape[0] // 2
    nq = wq_bf16.shape[1]
    n_prev = 0 if kt_prev is None else kt_prev.shape[0]
    prev_specs = [pl.BlockSpec((n_prev, None, tok_w, tm), lambda b, i: (0, b, 0, i))] * 2 if n_prev else []
    prev_args = [kt_prev, vt_prev] if n_prev else []
    stack_spec = pl.BlockSpec((n_prev + 1, None, tok_w, tm), lambda b, i: (0, b, 0, i))
    stack_shape = jax.ShapeDtypeStruct((n_prev + 1, n_seq, tok_w, seq_len), F32)
    return pl.pallas_call(
        functools.partial(_sb_proj_kernel, tok_w=tok_w, n_prev=n_prev),
        grid=(n_seq, nt),
        in_specs=[
            pl.BlockSpec((tm, d), lambda b, i: (b * nt + i, 0)),
            pl.BlockSpec((1, d), lambda b, i: (0, 0)),
            pl.BlockSpec((d, nq), lambda b, i: (0, 0)),
            pl.BlockSpec((2 * tok_w, d), lambda b, i: (0, 0)),
        ] + prev_specs,
        out_specs=[
            pl.BlockSpec((tm, tok_w), lambda b, i: (b * nt + i, 0)),
            pl.BlockSpec((tm, nq - tok_w), lambda b, i: (b * nt + i, 0)),
            stack_spec, stack_spec,
        ],
        out_shape=[
            jax.ShapeDtypeStruct((t, tok_w), F32),
            jax.ShapeDtypeStruct((t, nq - tok_w), F32),
            stack_shape, stack_shape,
        ],
        compiler_params=_cparams(("parallel", "parallel")),
        name="sb_proj",
    )(x, g.reshape(1, d), wq_bf16, wkvt_bf16, *prev_args)


def _mem_kv_kernel(x_ref, g_ref, wt_ref, k_ref, v_ref, h_scr):
    b = pl.program_id(1)

    @pl.when(pl.program_id(0) == 0)
    def _():
        h_scr[b] = _rmsnorm(x_ref[...], g_ref[...]).astype(BF16)

    r = _dot_nt(wt_ref[...], h_scr[b])
    k_ref[...] = r[:MEM_WIDTH]
    v_ref[...] = r[MEM_WIDTH:]


def mem_kv_all(mem, g, wt_bf16):
    n_seq, n_mem, d = mem.shape
    depth = wt_bf16.shape[0]
    return pl.pallas_call(
        _mem_kv_kernel,
        grid=(depth, n_seq),
        in_specs=[
            pl.BlockSpec((None, n_mem, d), lambda l, b: (b, 0, 0)),
            pl.BlockSpec((1, d), lambda l, b: (0, 0)),
            pl.BlockSpec((None, 2 * MEM_WIDTH, d), lambda l, b: (l, 0, 0)),
        ],
        out_specs=[pl.BlockSpec((None, None, MEM_WIDTH, n_mem), lambda l, b: (l, b, 0, 0))] * 2,
        out_shape=[jax.ShapeDtypeStruct((depth, n_seq, MEM_WIDTH, n_mem), F32)] * 2,
        scratch_shapes=[pltpu.VMEM((n_seq, n_mem, d), BF16)],
        compiler_params=_cparams(("arbitrary", "arbitrary")),
        name="mem_kv",
    )(mem, g.reshape(1, d), wt_bf16)


def _out_mlp_kernel(x_ref, mix_ref, mem_ref, w1_ref, w2_ref, g_ref, wu_ref, wd_ref, gf_ref, *refs,
                    nf, final_norm):
    if final_norm:
        o_ref, y_ref, h_scr, acc_scr = refs
    else:
        o_ref, h_scr, acc_scr = refs
    f = pl.program_id(1)

    @pl.when(f == 0)
    def _():
        proj = _dot(mix_ref[...].astype(BF16), w1_ref[...]) + _dot(mem_ref[...].astype(BF16), w2_ref[...])
        x1 = x_ref[...] + proj
        h_scr[...] = _rmsnorm(x1, g_ref[...]).astype(BF16)
        acc_scr[...] = x1

    a = _dot(h_scr[...], wu_ref[...])
    a = jnp.square(jnp.maximum(a, 0.0)).astype(BF16)
    acc_scr[...] += _dot(a, wd_ref[...])

    @pl.when(f == nf - 1)
    def _():
        x2 = acc_scr[...]
        o_ref[...] = x2
        if final_norm:
            y_ref[...] = _rmsnorm(x2, gf_ref[...])


def out_mlp(x, mix, mem_o, w_out_bf16, g, wu_bf16, wd_bf16, g_final, tm, tf, final_norm):
    t, d = x.shape
    n1 = mix.shape[1]
    n2 = mem_o.shape[1]
    ff = wu_bf16.shape[1]
    nf = ff // tf
    n_out = 2 if final_norm else 1
    outs = pl.pallas_call(
        functools.partial(_out_mlp_kernel, nf=nf, final_norm=final_norm),
        grid=(t // tm, nf),
        in_specs=[
            pl.BlockSpec((tm, d), lambda i, f: (i, 0)),
            pl.BlockSpec((tm, n1), lambda i, f: (i, 0)),
            pl.BlockSpec((tm, n2), lambda i, f: (i, 0)),
            pl.BlockSpec((n1, d), lambda i, f: (0, 0)),
            pl.BlockSpec((n2, d), lambda i, f: (0, 0)),
            pl.BlockSpec((1, d), lambda i, f: (0, 0)),
            pl.BlockSpec((d, tf), lambda i, f: (0, f)),
            pl.BlockSpec((tf, d), lambda i, f: (f, 0)),
            pl.BlockSpec((1, d), lambda i, f: (0, 0)),
        ],
        out_specs=[pl.BlockSpec((tm, d), lambda i, f: (i, 0))] * n_out,
        out_shape=[jax.ShapeDtypeStruct((t, d), F32)] * n_out,
        scratch_shapes=[pltpu.VMEM((tm, d), BF16), pltpu.VMEM((tm, d), F32)],
        compiler_params=_cparams(("parallel", "arbitrary")),
        name="out_mlp",
    )(x, mix, mem_o, w_out_bf16[:n1], w_out_bf16[n1:], g.reshape(1, d), wu_bf16, wd_bf16,
      g_final.reshape(1, d))
    return outs if final_norm else (outs[0], None)


def _glu_kernel(y_ref, w_ref, o_ref):
    g = jax.nn.gelu(y_ref[...])
    gm = _dot(g.astype(BF16), w_ref[...])
    o_ref[...] = g * jax.nn.sigmoid(gm)


def glu(y, w_bf16, tm):
    t, n = y.shape
    return pl.pallas_call(
        _glu_kernel,
        grid=(t // tm,),
        in_specs=[pl.BlockSpec((tm, n), lambda i: (i, 0)), pl.BlockSpec((n, n), lambda i: (0, 0))],
        out_specs=pl.BlockSpec((tm, n), lambda i: (i, 0)),
        out_shape=jax.ShapeDtypeStruct((t, n), F32),
        compiler_params=_cparams(("parallel",)),
        name="glu",
    )(y, w_bf16)


def _mem_attn_kernel(q_ref, kt_ref, vt_ref, o_ref):
    q = q_ref[...] * ATTN_SCALE
    kt = kt_ref[...].astype(BF16)
    vt = vt_ref[...].astype(BF16)
    lane_head = lax.broadcasted_iota(jnp.int32, (1, MEM_WIDTH), 1) // HEAD_DIM
    out = jnp.zeros(q.shape, F32)
    for h in range(MEM_HEADS):
        m = lane_head == h
        s = _dot(jnp.where(m, q, 0.0).astype(BF16), kt)
        p = jnp.exp(s - jnp.max(s, axis=-1, keepdims=True))
        p = p / jnp.sum(p, axis=-1, keepdims=True)
        out = out + jnp.where(m, _dot_nt(p.astype(BF16), vt), 0.0)
    o_ref[...] = out


def mem_attn(qm, mkt, mvt, layer, n_seq, tm):
    t = qm.shape[0]
    seq_len = t // n_seq
    n_mem = mkt.shape[-1]
    nt = seq_len // tm
    kv_spec = pl.BlockSpec((None, None, MEM_WIDTH, n_mem), lambda b, i: (layer, b, 0, 0))
    return pl.pallas_call(
        _mem_attn_kernel,
        grid=(n_seq, nt),
        in_specs=[pl.BlockSpec((tm, MEM_WIDTH), lambda b, i: (b * nt + i, 0)), kv_spec, kv_spec],
        out_specs=pl.BlockSpec((tm, MEM_WIDTH), lambda b, i: (b * nt + i, 0)),
        out_shape=jax.ShapeDtypeStruct((t, MEM_WIDTH), F32),
        compiler_params=_cparams(("parallel", "parallel")),
        name="mem_attn",
    )(qm, mkt, mvt)


def _mem_attn_short_kernel(q_ref, kt_ref, vt_ref, o_ref, *, n_q):
    n_b = kt_ref.shape[0]
    rows = MEM_HEADS * n_q
    row = lax.broadcasted_iota(jnp.int32, (rows, 1), 0)
    head_mask = (lax.broadcasted_iota(jnp.int32, (1, MEM_WIDTH), 1) // HEAD_DIM) == (row // n_q)
    for b in range(n_b):
        q = q_ref[b * n_q:(b + 1) * n_q, :] * ATTN_SCALE
        qbd = jnp.where(head_mask, jnp.concatenate([q] * MEM_HEADS, axis=0), 0.0).astype(BF16)
        s = _dot(qbd, kt_ref[b].astype(BF16))
        p = jnp.exp(s - jnp.max(s, axis=-1, keepdims=True))
        p = p / jnp.sum(p, axis=-1, keepdims=True)
        a = jnp.where(head_mask, _dot_nt(p.astype(BF16), vt_ref[b].astype(BF16)), 0.0)
        out = a[0:n_q]
        for h in range(1, MEM_HEADS):
            out = out + a[h * n_q:(h + 1) * n_q]
        o_ref[b * n_q:(b + 1) * n_q, :] = out


def mem_attn_short(qm, mkt, mvt, layer, n_seq, n_b):
    t = qm.shape[0]
    n_q = t // n_seq
    n_mem = mkt.shape[-1]
    kv_spec = pl.BlockSpec((None, n_b, MEM_WIDTH, n_mem), lambda i: (layer, i, 0, 0))
    return pl.pallas_call(
        functools.partial(_mem_attn_short_kernel, n_q=n_q),
        grid=(n_seq // n_b,),
        in_specs=[pl.BlockSpec((n_b * n_q, MEM_WIDTH), lambda i: (i, 0)), kv_spec, kv_spec],
        out_specs=pl.BlockSpec((n_b * n_q, MEM_WIDTH), lambda i: (i, 0)),
        out_shape=jax.ShapeDtypeStruct((t, MEM_WIDTH), F32),
        compiler_params=_cparams(("parallel",)),
        name="mem_attn_short",
    )(qm, mkt, mvt)


def _ssm_prep_kernel(lr_ref, li_ref, ls_ref, lrc_ref, lic_ref, lsc_ref, btr_ref, bti_ref, ctr_ref, cti_ref,
                     m_ref, bc_ref, cc_ref, apr_ref, api_ref, hpr_ref, hpi_ref,
                     hc_scr, d_scr, pr_scr, pi_scr, bbr_scr, bbi_scr, *, lc, n_pow):
    s = pl.program_id(1)

    def discretise(lam_re, lam_im, log_step):
        a_re = jnp.minimum(lam_re, -1e-4)
        step = jnp.exp(log_step)
        mag = jnp.exp(step * a_re)
        return a_re, lam_im, mag * jnp.cos(step * lam_im), mag * jnp.sin(step * lam_im)

    @pl.when(s == 0)
    def _():
        a_re, a_im, lbr, lbi = discretise(lr_ref[...], li_ref[...], ls_ref[...])
        den = a_re * a_re + a_im * a_im
        nr, ni = _cmul(lbr - 1.0, lbi, a_re, -a_im)
        bbr, bbi = _cmul(nr / den, ni / den, btr_ref[...], bti_ref[...])
        bbr_scr[...] = bbr
        bbi_scr[...] = bbi
        pr, pi = jnp.ones_like(lbr), jnp.zeros_like(lbi)
        for j in range(lc + 1):
            pr_scr[j] = pr
            pi_scr[j] = pi
            if j == lc // 2:
                hpr_ref[...] = pr
                hpi_ref[...] = pi
            if j == lc:
                ar, ai = pr, pi
                for k in range(n_pow):
                    apr_ref[k:k + 1, :] = ar
                    api_ref[k:k + 1, :] = ai
                    ar, ai = _cmul(ar, ai, ar, ai)
            pr, pi = _cmul(pr, pi, lbr, lbi)
        _, _, lbrc, lbic = discretise(lrc_ref[...], lic_ref[...], lsc_ref[...])
        shape = ctr_ref.shape
        lbrc = jnp.broadcast_to(lbrc, shape)
        lbic = jnp.broadcast_to(lbic, shape)
        qr, qi = ctr_ref[...], cti_ref[...]
        bcat = jnp.concatenate([bbr, bbi], axis=1)
        for j in range(lc + 1):
            hj = jnp.concatenate([qr, -qi], axis=0)
            hc_scr[j] = hj
            if j < lc:
                d_scr[j] = jnp.dot(bcat, hj, precision=lax.Precision.HIGHEST, preferred_element_type=F32)
            qr, qi = _cmul(qr, qi, lbrc, lbic)

    pr, pi = pr_scr[s], pi_scr[s]
    gr, gi = _cmul(bbr_scr[...], bbi_scr[...], pr, pi)
    bc_ref[...] = jnp.concatenate([gr, gi], axis=1).astype(BF16)
    cc_ref[...] = hc_scr[s + 1].astype(BF16)
    for t in range(lc):
        blk = d_scr[jnp.maximum(t - s, 0)]
        m_ref[:, t * LANES:(t + 1) * LANES] = jnp.where(t >= s, blk, 0.0).astype(BF16)


def ssm_operators(lam_re, lam_im, log_step, b_re, b_im, c_re, c_im, lc, n_pow):
    n_g, n_p, n_c = b_re.shape
    n_slab = n_g // SLAB_GROUPS
    same = jnp.eye(SLAB_GROUPS, dtype=bool)

    def rows(x):
        return x.astype(F32).reshape(n_slab, 1, SLAB_STATE)

    def cols(x):
        return x.astype(F32).reshape(n_slab, SLAB_STATE, 1)

    def bt_blocks(b):
        bt = jnp.transpose(b.astype(F32), (0, 2, 1)).reshape(n_slab, SLAB_GROUPS, n_c, 1, n_p)
        bt = jnp.where(same[None, :, None, :, None], bt, 0.0)
        return bt.reshape(n_slab, LANES, SLAB_STATE)

    def ct_blocks(c):
        ct = jnp.transpose(c.astype(F32), (0, 2, 1)).reshape(n_slab, SLAB_GROUPS, n_p, 1, n_c)
        ct = jnp.where(same[None, :, None, :, None], ct, 0.0)
        return ct.reshape(n_slab, SLAB_STATE, LANES)

    step_gp = jnp.broadcast_to(log_step.astype(F32)[:, None], (n_g, n_p))
    row_spec = pl.BlockSpec((None, 1, SLAB_STATE), lambda g, s: (g, 0, 0))
    col_spec = pl.BlockSpec((None, SLAB_STATE, 1), lambda g, s: (g, 0, 0))
    bt_spec = pl.BlockSpec((None, LANES, SLAB_STATE), lambda g, s: (g, 0, 0))
    ct_spec = pl.BlockSpec((None, SLAB_STATE, LANES), lambda g, s: (g, 0, 0))
    pow_spec = pl.BlockSpec((None, n_pow, SLAB_STATE), lambda g, s: (g, 0, 0))
    kdim = lc * LANES
    half_shape = jax.ShapeDtypeStruct((n_slab, 1, SLAB_STATE), F32)
    return pl.pallas_call(
        functools.partial(_ssm_prep_kernel, lc=lc, n_pow=n_pow),
        grid=(n_slab, lc),
        in_specs=[row_spec] * 3 + [col_spec] * 3 + [bt_spec] * 2 + [ct_spec] * 2,
        out_specs=[
            pl.BlockSpec((None, LANES, kdim), lambda g, s: (g, s, 0)),
            pl.BlockSpec((None, LANES, 2 * SLAB_STATE), lambda g, s: (g, lc - 1 - s, 0)),
            pl.BlockSpec((None, 2 * SLAB_STATE, LANES), lambda g, s: (g, 0, s)),
            pow_spec, pow_spec, row_spec, row_spec,
        ],
        out_shape=[
            jax.ShapeDtypeStruct((n_slab, kdim, kdim), BF16),
            jax.ShapeDtypeStruct((n_slab, kdim, 2 * SLAB_STATE), BF16),
            jax.ShapeDtypeStruct((n_slab, 2 * SLAB_STATE, kdim), BF16),
            jax.ShapeDtypeStruct((n_slab, n_pow, SLAB_STATE), F32),
            jax.ShapeDtypeStruct((n_slab, n_pow, SLAB_STATE), F32),
            half_shape, half_shape,
        ],
        scratch_shapes=[
            pltpu.VMEM((lc + 1, 2 * SLAB_STATE, LANES), F32),
            pltpu.VMEM((lc, LANES, LANES), F32),
            pltpu.VMEM((lc + 1, 1, SLAB_STATE), F32),
            pltpu.VMEM((lc + 1, 1, SLAB_STATE), F32),
            pltpu.VMEM((LANES, SLAB_STATE), F32),
            pltpu.VMEM((LANES, SLAB_STATE), F32),
        ],
        compiler_params=_cparams(("parallel", "arbitrary")),
        name="ssm_prep",
    )(rows(lam_re), rows(lam_im), rows(step_gp), cols(lam_re), cols(lam_im), cols(step_gp),
      bt_blocks(b_re), bt_blocks(b_im), ct_blocks(c_re), ct_blocks(c_im))


def _ssm_kernel(u_ref, x0r_ref, x0i_ref, m_ref, bc_ref, cc_ref, d_ref, apr_ref, api_ref,
                y_ref, xr_ref, xi_ref, *, lc, nc):
    rows = u_ref.shape[0] // lc
    nb = rows // nc
    us = [u_ref[pl.ds(s, rows, stride=lc), :] for s in range(lc)]
    ucat = jnp.concatenate([u.astype(BF16) for u in us], axis=1)
    grp = MXU_DIM // LANES
    y = jnp.concatenate(
        [_dot(ucat[:, :(j + 1) * MXU_DIM], m_ref[:(j + 1) * MXU_DIM, j * MXU_DIM:(j + 1) * MXU_DIM])
         for j in range(lc // grp)], axis=1)
    v = _dot(ucat, bc_ref[...])
    vr, vi = v[:, :SLAB_STATE], v[:, SLAB_STATE:]
    x0r, x0i = x0r_ref[...], x0i_ref[...]
    inj_r, inj_i = _cmul(apr_ref[0:1, :], api_ref[0:1, :], x0r, x0i)
    if nc == 1:
        xr, xi = vr + inj_r, vi + inj_i
        pr, pi = x0r, x0i
        xr_ref[...] = xr
        xi_ref[...] = xi
    else:
        row = lax.broadcasted_iota(jnp.int32, (rows, 1), 0)
        chunk = row % nc

        def first_rows(x):
            out = jnp.zeros((rows, x.shape[1]), F32)
            for b in range(nb):
                out = jnp.where(row == b * nc, x[b:b + 1, :], out)
            return out

        vr = vr + first_rows(inj_r)
        vi = vi + first_rows(inj_i)
        for k in range(nc.bit_length() - 1):
            sh = 1 << k
            sr = jnp.where(chunk >= sh, pltpu.roll(vr, sh, 0), 0.0)
            si = jnp.where(chunk >= sh, pltpu.roll(vi, sh, 0), 0.0)
            dr, di = _cmul(apr_ref[k:k + 1, :], api_ref[k:k + 1, :], sr, si)
            vr, vi = vr + dr, vi + di
        for b in range(nb):
            last = (b + 1) * nc - 1
            xr_ref[b:b + 1, :] = vr[last:last + 1, :]
            xi_ref[b:b + 1, :] = vi[last:last + 1, :]
        pr = jnp.where(chunk >= 1, pltpu.roll(vr, 1, 0), first_rows(x0r))
        pi = jnp.where(chunk >= 1, pltpu.roll(vi, 1, 0), first_rows(x0i))
    prev = jnp.concatenate([pr.astype(BF16), pi.astype(BF16)], axis=1)
    y = y + _dot(prev, cc_ref[...])
    d = d_ref[...]
    for s in range(lc):
        y_ref[pl.ds(s, rows, stride=lc), :] = y[:, s * LANES:(s + 1) * LANES] + d * us[s]


def ssm_apply(u, x0r, x0i, ops, d_skip, lc, nc, nb):
    m, bc, cc, apr, api = ops
    bc_blk = bc.shape[1] // (lc * LANES) - 1
    t, width = u.shape
    n_slab = width // LANES
    tok = nb * nc * lc
    n_blk = t // tok
    kdim = lc * LANES
    n_pow = apr.shape[1]
    state_spec = pl.BlockSpec((None, None, nb, SLAB_STATE), lambda s, b: (s, b, 0, 0))
    pow_spec = pl.BlockSpec((None, n_pow, SLAB_STATE), lambda s, b: (s, 0, 0))
    state_shape = jax.ShapeDtypeStruct((n_slab, n_blk, nb, SLAB_STATE), F32)
    return pl.pallas_call(
        functools.partial(_ssm_kernel, lc=lc, nc=nc),
        grid=(n_slab, n_blk),
        in_specs=[
            pl.BlockSpec((tok, LANES), lambda s, b: (b, s)),
            state_spec, state_spec,
            pl.BlockSpec((None, kdim, kdim), lambda s, b: (s, 0, 0)),
            pl.BlockSpec((None, kdim, 2 * SLAB_STATE), lambda s, b: (s, bc_blk, 0)),
            pl.BlockSpec((None, 2 * SLAB_STATE, kdim), lambda s, b: (s, 0, 0)),
            pl.BlockSpec((None, 1, LANES), lambda s, b: (s, 0, 0)),
            pow_spec, pow_spec,
        ],
        out_specs=[pl.BlockSpec((tok, LANES), lambda s, b: (b, s)), state_spec, state_spec],
        out_shape=[jax.ShapeDtypeStruct((t, width), F32), state_shape, state_shape],
        compiler_params=_cparams(("arbitrary", "arbitrary")),
        name="ssm",
    )(u, x0r, x0i, m, bc, cc, d_skip.astype(F32).reshape(n_slab, 1, LANES), apr, api)


def _state_to_slabs(x, nb):
    n_seq, n_g, n_p = x.shape
    n_slab = n_g // SLAB_GROUPS
    x = x.reshape(n_seq, n_slab, SLAB_STATE)
    return jnp.transpose(x, (1, 0, 2)).reshape(n_slab, n_seq // nb, nb, SLAB_STATE)


def _slabs_to_state(x, n_p):
    n_slab, n_blk, nb, _ = x.shape
    x = jnp.transpose(x.reshape(n_slab, n_blk * nb, SLAB_STATE), (1, 0, 2))
    return x.reshape(n_blk * nb, n_slab * SLAB_GROUPS, n_p)


def _sb_weights(qh, kt_blk, bias, tri, carry, mask):
    n_keys = kt_blk.shape[1]
    n_tri = tri.shape[0]
    z = _dot(qh, kt_blk) + bias
    sp = jnp.maximum(z, 0.0) + jnp.log(1.0 + jnp.exp2(jnp.abs(z) * -LOG2_E))
    if mask is not None:
        sp = jnp.where(mask, sp, 0.0)
    spb = sp.astype(BF16)
    parts = []
    for c in reversed(range(n_keys // n_tri)):
        sl = slice(c * n_tri, (c + 1) * n_tri)
        parts.insert(0, _dot(spb[:, sl], tri) + carry)
        carry = carry + jnp.sum(sp[:, sl], axis=-1, keepdims=True)
    suffix = parts[0] if len(parts) == 1 else jnp.concatenate(parts, axis=1)
    w = jnp.exp2((z - sp - suffix) * LOG2_E)
    if mask is not None:
        w = jnp.where(mask, w, 0.0)
    return w.astype(BF16), carry


def _sb_prompt_kernel(bias_ref, q_ref, kt_ref, vt_ref, tri_ref, o_ref, *, tq):
    pair = pl.program_id(1)
    i = pl.program_id(2)
    n_h = LANES // HEAD_DIM
    q = q_ref[...] * ATTN_SCALE
    tri = tri_ref[...]
    lane_head = lax.broadcasted_iota(jnp.int32, (1, LANES), 1) // HEAD_DIM
    qhs = [jnp.where(lane_head == hh, q, 0.0).astype(BF16) for hh in range(n_h)]
    biases = [bias_ref[pair * n_h + hh] for hh in range(n_h)]

    def load(start):
        return kt_ref[:, pl.ds(start, tq)].astype(BF16), vt_ref[:, pl.ds(start, tq)].astype(BF16)

    def diag_blocks(start):
        kt, vt = load(start)
        half = tq // 2
        def causal(n_keys, q_offset):
            qpos = lax.broadcasted_iota(jnp.int32, (half, n_keys), 0)
            kpos = lax.broadcasted_iota(jnp.int32, (half, n_keys), 1)
            return kpos < qpos + q_offset

        mask_top = causal(half, 0)
        mask_bot = causal(tq, half)
        zero = jnp.zeros((half, 1), F32)
        res = []
        for hh in range(n_h):
            w_t, c_t = _sb_weights(qhs[hh][:half], kt[:, :half], biases[hh], tri, zero, mask_top)
            w_b, c_b = _sb_weights(qhs[hh][half:], kt, biases[hh], tri, zero, mask_bot)
            o = jnp.concatenate([_dot_nt(w_t, vt[:, :half]), _dot_nt(w_b, vt)], axis=0)
            lanes0 = jnp.zeros((half, LANES), F32)
            c = jnp.concatenate([c_t + lanes0, c_b + lanes0], axis=0)
            res.append((o, jnp.max(c, axis=1, keepdims=True)))
        return res

    res = diag_blocks(pl.multiple_of(i * tq, tq))

    def body(it, carry):
        kt, vt = load(pl.multiple_of((i - 1 - it) * tq, tq))
        res = []
        for hh in range(n_h):
            w, c = _sb_weights(qhs[hh], kt, biases[hh], tri, carry[hh][1], None)
            res.append((carry[hh][0] + _dot_nt(w, vt), c))
        return tuple(res)

    res = lax.fori_loop(0, i, body, tuple(res))
    out = res[0][0]
    for hh in range(1, n_h):
        out = jnp.where(lane_head == hh, res[hh][0], out)
    o_ref[...] = out


def _tri(n):
    r = jnp.arange(n)
    return (r[:, None] > r[None, :]).astype(BF16)


def sb_prompt(q, kt, vt, layer, bias, tq):
    t, width = q.shape
    _, n_seq, _, seq_len = kt.shape
    nq = seq_len // tq
    n_pair = width // LANES
    kv_spec = pl.BlockSpec((None, None, LANES, seq_len), lambda b, p, i: (layer, b, p, 0))
    return pl.pallas_call(
        functools.partial(_sb_prompt_kernel, tq=tq),
        grid=(n_seq, n_pair, nq),
        in_specs=[
            pl.BlockSpec(memory_space=pltpu.SMEM),
            pl.BlockSpec((tq, LANES), lambda b, p, i: (b * nq + i, p)),
            kv_spec, kv_spec,
            pl.BlockSpec((MXU_DIM, MXU_DIM), lambda b, p, i: (0, 0)),
        ],
        out_specs=pl.BlockSpec((tq, LANES), lambda b, p, i: (b * nq + i, p)),
        out_shape=jax.ShapeDtypeStruct((t, width), F32),
        compiler_params=_cparams(("parallel", "parallel", "arbitrary")),
        name="sb_prompt",
    )(bias.astype(F32), q, kt, vt, _tri(MXU_DIM))


def _sb_sample_kernel(pt_ref, q_ref, knt_ref, vnt_ref, bias_ref, tri_ref, *refs, n_q, n_pg):
    k_refs = refs[:n_pg]
    v_refs = refs[n_pg:2 * n_pg]
    o_ref = refs[2 * n_pg]
    acc_scr, carry_scr = refs[2 * n_pg + 1:]
    i = pl.program_id(1)
    width = q_ref.shape[1]
    n_heads = width // HEAD_DIM
    rows = bias_ref.shape[0]
    row = lax.broadcasted_iota(jnp.int32, (rows, 1), 0)
    head_mask = (lax.broadcasted_iota(jnp.int32, (1, width), 1) // HEAD_DIM) == (row // n_q)
    q = q_ref[...] * ATTN_SCALE
    qrep = jnp.concatenate([q] * (rows // n_q), axis=0)
    qbd = jnp.where(head_mask, qrep, 0.0).astype(BF16)
    bias = bias_ref[...]
    tri = tri_ref[...]

    @pl.when(i == 0)
    def _():
        kpos = lax.broadcasted_iota(jnp.int32, (rows, PAGE_SIZE), 1)
        mask = kpos < (row % n_q)
        w, c = _sb_weights(qbd, knt_ref[...].astype(BF16), bias, tri[:PAGE_SIZE, :PAGE_SIZE],
                           jnp.zeros((rows, 1), F32), mask)
        acc_scr[...] = _dot_nt(vnt_ref[...].astype(BF16), w)
        carry_scr[...] = c

    acc = acc_scr[...]
    carry = carry_scr[...]
    per_blk = MXU_DIM // PAGE_SIZE
    for j in range(n_pg // per_blk):
        pages = [j * per_blk + r for r in reversed(range(per_blk))]
        kt = jnp.concatenate([k_refs[p][...] for p in pages], axis=1).astype(BF16)
        vt = jnp.concatenate([v_refs[p][...] for p in pages], axis=1).astype(BF16)
        w, carry = _sb_weights(qbd, kt, bias, tri, carry, None)
        acc = acc + _dot_nt(vt, w)
    acc_scr[...] = acc
    carry_scr[...] = carry

    @pl.when(i == pl.num_programs(1) - 1)
    def _():
        a = jnp.where(head_mask, jnp.transpose(acc), 0.0)
        out = a[0:n_q]
        for h in range(1, n_heads):
            out = out + a[h * n_q:(h + 1) * n_q]
        o_ref[...] = out


def sb_sample(q, knt, vnt, cache_kt, cache_vt, layer, page_table, bias, n_q, n_pg=16):
    t, width = q.shape
    n_seq = t // n_q
    n_pages = page_table.shape[1]
    rows = (width // HEAD_DIM) * n_q
    rows_pad = -(-rows // LANES) * LANES
    bias_col = jnp.pad(jnp.repeat(bias.astype(F32), n_q), (0, rows_pad - rows)).reshape(rows_pad, 1)

    def page_spec(r):
        return pl.BlockSpec((None, None, width, PAGE_SIZE),
                            lambda b, i, pt: (layer, pt[b, n_pages - 1 - (i * n_pg + r)], 0, 0))

    new_spec = pl.BlockSpec((None, width, PAGE_SIZE), lambda b, i, pt: (b, 0, 0))
    grid_spec = pltpu.PrefetchScalarGridSpec(
        num_scalar_prefetch=1,
        grid=(n_seq, n_pages // n_pg),
        in_specs=[
            pl.BlockSpec((n_q, width), lambda b, i, pt: (b, 0)),
            new_spec, new_spec,
            pl.BlockSpec((rows_pad, 1), lambda b, i, pt: (0, 0)),
            pl.BlockSpec((MXU_DIM, MXU_DIM), lambda b, i, pt: (0, 0)),
        ] + [page_spec(r) for r in range(n_pg)] * 2,
        out_specs=pl.BlockSpec((n_q, width), lambda b, i, pt: (b, 0)),
        scratch_shapes=[pltpu.VMEM((width, rows_pad), F32), pltpu.VMEM((rows_pad, 1), F32)],
    )
    return pl.pallas_call(
        functools.partial(_sb_sample_kernel, n_q=n_q, n_pg=n_pg),
        grid_spec=grid_spec,
        out_shape=jax.ShapeDtypeStruct((t, width), F32),
        compiler_params=_cparams(("parallel", "arbitrary")),
        name="sb_sample",
    )(page_table, q, knt, vnt, bias_col, _tri(MXU_DIM), *([cache_kt] * n_pg), *([cache_vt] * n_pg))


def _pick_tile(t, pref):
    while t % pref:
        pref //= 2
    return pref


def _feature_major_pages(cache):
    n_l, pool, page, heads, d = cache.shape
    return jnp.transpose(cache, (0, 1, 3, 4, 2)).reshape(n_l, pool, heads * d, page)


def _new_tokens_feature_major(x, n_seq):
    n_q = x.shape[0] // n_seq
    xt = jnp.transpose(x.reshape(n_seq, n_q, x.shape[1]), (0, 2, 1))
    return jnp.pad(xt, ((0, 0), (0, 0), (0, PAGE_SIZE - n_q)))


def kernel(x_prompt, x_sample, mem_prompt, cache_sb_k, cache_sb_v, page_table, cache_mem_k, cache_mem_v, state_ssm_re, state_ssm_im, norm_mix, norm_mlp, norm_mem, norm_final, w_in_ssm, ssm_lambda_re, ssm_lambda_im, ssm_log_step, ssm_b_re, ssm_b_im, ssm_c_re, ssm_c_im, ssm_d, ssm_w_glu, w_in_sb, sb_bias, w_mem_kv, w_out, w_up, w_down):
    n_p, seq_p, d_model = x_prompt.shape
    n_s, seq_s, _ = x_sample.shape
    depth = w_out.shape[0]
    n_mem = mem_prompt.shape[1]
    tok_w = ssm_d.shape[1]
    n_groups, n_state = ssm_lambda_re.shape[1:]
    sb_heads = tok_w // HEAD_DIM
    lc_p = 16
    nc_p = seq_p // lc_p

    xp = x_prompt.reshape(n_p * seq_p, d_model)
    xs = x_sample.reshape(n_s * seq_s, d_model)
    tp, ts = xp.shape[0], xs.shape[0]
    tm_p = _pick_tile(seq_p, 512)
    tm_s = _pick_tile(ts, 256)

    w_in_ssm_b = w_in_ssm.astype(BF16)
    w_in_sb_b = w_in_sb.astype(BF16)
    wq_sb_b = jnp.concatenate([w_in_sb_b[:, :, :tok_w], w_in_sb_b[:, :, 3 * tok_w:]], axis=2)
    wkvt_sb_b = jnp.transpose(w_in_sb_b[:, :, tok_w:3 * tok_w], (0, 2, 1))
    w_out_b = w_out.astype(BF16)
    w_up_b = w_up.astype(BF16)
    w_down_b = w_down.astype(BF16)
    w_glu_b = ssm_w_glu.astype(BF16)
    cache_kt = _feature_major_pages(cache_sb_k)
    cache_vt = _feature_major_pages(cache_sb_v)
    n_lm, _, _, mem_heads, _ = cache_mem_k.shape
    mem_kt_s = jnp.transpose(cache_mem_k, (0, 1, 3, 4, 2)).reshape(n_lm, n_s, MEM_WIDTH, n_mem)
    mem_vt_s = jnp.transpose(cache_mem_v, (0, 1, 3, 4, 2)).reshape(n_lm, n_s, MEM_WIDTH, n_mem)

    mem_kt, mem_vt = mem_kv_all(mem_prompt, norm_mem, jnp.transpose(w_mem_kv, (0, 2, 1)).astype(BF16))

    ssm_re_p, ssm_im_p, ssm_re_s, ssm_im_s = [], [], [], []
    sb_k_s, sb_v_s = [], []
    kt_p = vt_p = None
    y_p = y_s = None
    for i in range(depth):
        j = i // 2
        if i % 2 == 0:
            prm = (ssm_lambda_re[j], ssm_lambda_im[j], ssm_log_step[j], ssm_b_re[j], ssm_b_im[j],
                   ssm_c_re[j], ssm_c_im[j])
            m_op, bc_op, cc_op, apr, api, hpr, hpi = ssm_operators(*prm, lc=lc_p, n_pow=nc_p.bit_length() - 1)
            ops_p = (m_op, bc_op, cc_op, apr, api)
            if 2 * seq_s == lc_p:
                ops_s = (m_op, bc_op, cc_op, hpr, hpi)
            else:
                ops_s = ssm_operators(*prm, lc=seq_s, n_pow=1)[:5]
            u_p, qm_p = norm_proj(xp, norm_mix[i], w_in_ssm_b[j], (tok_w, MEM_WIDTH), tm_p)
            u_s, qm_s = norm_proj(xs, norm_mix[i], w_in_ssm_b[j], (tok_w, MEM_WIDTH), tm_s)
            zero = jnp.zeros((n_p, n_groups, n_state), F32)
            nb_p = 2 if n_p % 2 == 0 else 1
            yv_p, fr_p, fi_p = ssm_apply(u_p, _state_to_slabs(zero, nb_p), _state_to_slabs(zero, nb_p),
                                         ops_p, ssm_d[j], lc_p, nc_p, nb_p)
            yv_s, fr_s, fi_s = ssm_apply(u_s, _state_to_slabs(state_ssm_re[j].astype(F32), n_s),
                                         _state_to_slabs(state_ssm_im[j].astype(F32), n_s),
                                         ops_s, ssm_d[j], seq_s, 1, n_s)
            mix_p = glu(yv_p, w_glu_b[j], tm_p)
            mix_s = glu(yv_s, w_glu_b[j], tm_s)
            ssm_re_p.append(_slabs_to_state(fr_p, n_state))
            ssm_im_p.append(_slabs_to_state(fi_p, n_state))
            ssm_re_s.append(_slabs_to_state(fr_s, n_state))
            ssm_im_s.append(_slabs_to_state(fi_s, n_state))
        else:
            q_p, qm_p, kt_p, vt_p = sb_proj(xp, norm_mix[i], wq_sb_b[j], wkvt_sb_b[j], n_p, tm_p, kt_p, vt_p)
            q_s, k_s, v_s, qm_s = norm_proj(xs, norm_mix[i], w_in_sb_b[j], (tok_w, tok_w, tok_w, MEM_WIDTH), tm_s)
            mix_p = sb_prompt(q_p, kt_p, vt_p, j, sb_bias[j], 512)
            mix_s = sb_sample(q_s, _new_tokens_feature_major(k_s, n_s), _new_tokens_feature_major(v_s, n_s),
                              cache_kt, cache_vt, j, page_table, sb_bias[j], seq_s)
            sb_k_s.append(k_s.reshape(n_s, seq_s, sb_heads, HEAD_DIM))
            sb_v_s.append(v_s.reshape(n_s, seq_s, sb_heads, HEAD_DIM))
        mo_p = mem_attn(qm_p, mem_kt, mem_vt, i, n_p, tm_p)
        mo_s = mem_attn_short(qm_s, mem_kt_s, mem_vt_s, i, n_s, _pick_tile(n_s, 8))
        last = i == depth - 1
        xp, y_p = out_mlp(xp, mix_p, mo_p, w_out_b[i], norm_mlp[i], w_up_b[i], w_down_b[i], norm_final,
                          _pick_tile(tp, 1024), 512, last)
        xs, y_s = out_mlp(xs, mix_s, mo_s, w_out_b[i], norm_mlp[i], w_up_b[i], w_down_b[i], norm_final,
                          tm_s, 1024, last)

    def token_major(xt):
        n_l, n_seq, _, n_pos = xt.shape
        return jnp.transpose(xt.reshape(n_l, n_seq, -1, HEAD_DIM, n_pos), (0, 1, 4, 2, 3))

    return (y_p.reshape(n_p, seq_p, d_model), y_s.reshape(n_s, seq_s, d_model),
            jnp.stack(ssm_re_p), jnp.stack(ssm_im_p), jnp.stack(ssm_re_s), jnp.stack(ssm_im_s),
            token_major(kt_p), token_major(vt_p),
            jnp.stack(sb_k_s), jnp.stack(sb_v_s),
            token_major(mem_kt), token_major(mem_vt))
```

```python
import functools
import math

import jax
import jax.numpy as jnp
from jax import lax
from jax.experimental import pallas as pl
from jax.experimental.pallas import tpu as pltpu

F32 = jnp.float32
BF16 = jnp.bfloat16

RMS_EPS = 1e-6
HEAD_DIM = 64
ATTN_SCALE = 1.0 / math.sqrt(HEAD_DIM)
LOG2_E = 1.0 / math.log(2.0)
MEM_HEADS = 4
MEM_WIDTH = MEM_HEADS * HEAD_DIM
SSM_GROUP_CH = 16
SSM_STATE = 64
PAGE_SIZE = 128

LANES = 128
MXU_DIM = 256
SLAB_GROUPS = LANES // SSM_GROUP_CH
SLAB_STATE = SLAB_GROUPS * SSM_STATE
VMEM_LIMIT = 56 * 1024 * 1024


def _cparams(sem):
    return pltpu.CompilerParams(dimension_semantics=sem, vmem_limit_bytes=VMEM_LIMIT)


def _rmsnorm(x, g):
    ms = jnp.mean(x * x, axis=-1, keepdims=True)
    return x * lax.rsqrt(ms + RMS_EPS) * g


def _dot(a, b):
    return jnp.dot(a, b, preferred_element_type=F32)


def _dot_nt(a, b):
    return lax.dot_general(a, b, (((1,), (1,)), ((), ())), preferred_element_type=F32)


def _cmul(ar, ai, br, bi):
    return ar * br - ai * bi, ar * bi + ai * br


def _norm_proj_kernel(x_ref, g_ref, w_ref, *o_refs, widths):
    h = _rmsnorm(x_ref[...], g_ref[...]).astype(BF16)
    r = _dot(h, w_ref[...])
    off = 0
    for o_ref, wd in zip(o_refs, widths):
        o_ref[...] = r[:, off:off + wd]
        off += wd


def norm_proj(x, g, w_bf16, widths, tm):
    t, d = x.shape
    n = w_bf16.shape[1]
    assert sum(widths) == n and t % tm == 0
    return pl.pallas_call(
        functools.partial(_norm_proj_kernel, widths=widths),
        grid=(t // tm,),
        in_specs=[
            pl.BlockSpec((tm, d), lambda i: (i, 0)),
            pl.BlockSpec((1, d), lambda i: (0, 0)),
            pl.BlockSpec((d, n), lambda i: (0, 0)),
        ],
        out_specs=[pl.BlockSpec((tm, wd), lambda i: (i, 0)) for wd in widths],
        out_shape=[jax.ShapeDtypeStruct((t, wd), F32) for wd in widths],
        compiler_params=_cparams(("parallel",)),
        name="norm_proj",
    )(x, g.reshape(1, d), w_bf16)


def _sb_proj_kernel(x_ref, g_ref, wq_ref, wkvt_ref, *refs, tok_w, n_prev):
    if n_prev:
        kt_prev_ref, vt_prev_ref, q_ref, qm_ref, kt_ref, vt_ref = refs
        kt_ref[:n_prev] = kt_prev_ref[...]
        vt_ref[:n_prev] = vt_prev_ref[...]
    else:
        q_ref, qm_ref, kt_ref, vt_ref = refs
    h = _rmsnorm(x_ref[...], g_ref[...]).astype(BF16)
    r = _dot(h, wq_ref[...])
    q_ref[...] = r[:, :tok_w]
    qm_ref[...] = r[:, tok_w:]
    kv = _dot_nt(wkvt_ref[...], h)
    kt_ref[n_prev] = kv[:tok_w]
    vt_ref[n_prev] = kv[tok_w:]


def sb_proj(x, g, wq_bf16, wkvt_bf16, n_seq, tm, kt_prev=None, vt_prev=None):
    t, d = x.shape
    seq_len = t // n_seq
    nt = seq_len // tm
    tok_w = wkvt_bf16.shape[0] // 2
    nq = wq_bf16.shape[1]
    n_prev = 0 if kt_prev is None else kt_prev.shape[0]
    prev_specs = [pl.BlockSpec((n_prev, None, tok_w, tm), lambda b, i: (0, b, 0, i))] * 2 if n_prev else []
    prev_args = [kt_prev, vt_prev] if n_prev else []
    stack_spec = pl.BlockSpec((n_prev + 1, None, tok_w, tm), lambda b, i: (0, b, 0, i))
    stack_shape = jax.ShapeDtypeStruct((n_prev + 1, n_seq, tok_w, seq_len), F32)
    return pl.pallas_call(
        functools.partial(_sb_proj_kernel, tok_w=tok_w, n_prev=n_prev),
        grid=(n_seq, nt),
        in_specs=[
            pl.BlockSpec((tm, d), lambda b, i: (b * nt + i, 0)),
            pl.BlockSpec((1, d), lambda b, i: (0, 0)),
            pl.BlockSpec((d, nq), lambda b, i: (0, 0)),
            pl.BlockSpec((2 * tok_w, d), lambda b, i: (0, 0)),
        ] + prev_specs,
        out_specs=[
            pl.BlockSpec((tm, tok_w), lambda b, i: (b * nt + i, 0)),
            pl.BlockSpec((tm, nq - tok_w), lambda b, i: (b * nt + i, 0)),
            stack_spec, stack_spec,
        ],
        out_shape=[
            jax.ShapeDtypeStruct((t, tok_w), F32),
            jax.ShapeDtypeStruct((t, nq - tok_w), F32),
            stack_shape, stack_shape,
        ],
        compiler_params=_cparams(("parallel", "parallel")),
        name="sb_proj",
    )(x, g.reshape(1, d), wq_bf16, wkvt_bf16, *prev_args)


def _mem_kv_kernel(x_ref, g_ref, wt_ref, k_ref, v_ref, h_scr):
    b = pl.program_id(1)

    @pl.when(pl.program_id(0) == 0)
    def _():
        h_scr[b] = _rmsnorm(x_ref[...], g_ref[...]).astype(BF16)

    r = _dot_nt(wt_ref[...], h_scr[b])
    k_ref[...] = r[:MEM_WIDTH]
    v_ref[...] = r[MEM_WIDTH:]


def mem_kv_all(mem, g, wt_bf16):
    n_seq, n_mem, d = mem.shape
    depth = wt_bf16.shape[0]
    return pl.pallas_call(
        _mem_kv_kernel,
        grid=(depth, n_seq),
        in_specs=[
            pl.BlockSpec((None, n_mem, d), lambda l, b: (b, 0, 0)),
            pl.BlockSpec((1, d), lambda l, b: (0, 0)),
            pl.BlockSpec((None, 2 * MEM_WIDTH, d), lambda l, b: (l, 0, 0)),
        ],
        out_specs=[pl.BlockSpec((None, None, MEM_WIDTH, n_mem), lambda l, b: (l, b, 0, 0))] * 2,
        out_shape=[jax.ShapeDtypeStruct((depth, n_seq, MEM_WIDTH, n_mem), F32)] * 2,
        scratch_shapes=[pltpu.VMEM((n_seq, n_mem, d), BF16)],
        compiler_params=_cparams(("arbitrary", "arbitrary")),
        name="mem_kv",
    )(mem, g.reshape(1, d), wt_bf16)


def _out_mlp_kernel(x_ref, mix_ref, mem_ref, w1_ref, w2_ref, g_ref, wu_ref, wd_ref, gf_ref, *refs,
                    nf, final_norm):
    if final_norm:
        o_ref, y_ref, h_scr, acc_scr = refs
    else:
        o_ref, h_scr, acc_scr = refs
    f = pl.program_id(1)

    @pl.when(f == 0)
    def _():
        proj = _dot(mix_ref[...].astype(BF16), w1_ref[...]) + _dot(mem_ref[...].astype(BF16), w2_ref[...])
        x1 = x_ref[...] + proj
        h_scr[...] = _rmsnorm(x1, g_ref[...]).astype(BF16)
        acc_scr[...] = x1

    a = _dot(h_scr[...], wu_ref[...])
    a = jnp.square(jnp.maximum(a, 0.0)).astype(BF16)
    acc_scr[...] += _dot(a, wd_ref[...])

    @pl.when(f == nf - 1)
    def _():
        x2 = acc_scr[...]
        o_ref[...] = x2
        if final_norm:
            y_ref[...] = _rmsnorm(x2, gf_ref[...])


def out_mlp(x, mix, mem_o, layer, w_out_bf16, g, wu_bf16, wd_bf16, g_final, tm, tf, final_norm):
    t, d = x.shape
    n1 = mix.shape[1]
    n2 = mem_o.shape[1]
    assert n1 % n2 == 0
    ff = wu_bf16.shape[2]
    nf = ff // tf
    n_out = 2 if final_norm else 1
    outs = pl.pallas_call(
        functools.partial(_out_mlp_kernel, nf=nf, final_norm=final_norm),
        grid=(t // tm, nf),
        in_specs=[
            pl.BlockSpec((tm, d), lambda i, f: (i, 0)),
            pl.BlockSpec((tm, n1), lambda i, f: (i, 0)),
            pl.BlockSpec((tm, n2), lambda i, f: (i, 0)),
            pl.BlockSpec((None, n1, d), lambda i, f: (layer, 0, 0)),
            pl.BlockSpec((None, n2, d), lambda i, f: (layer, n1 // n2, 0)),
            pl.BlockSpec((1, d), lambda i, f: (0, 0)),
            pl.BlockSpec((None, d, tf), lambda i, f: (layer, 0, f)),
            pl.BlockSpec((None, tf, d), lambda i, f: (layer, f, 0)),
            pl.BlockSpec((1, d), lambda i, f: (0, 0)),
        ],
        out_specs=[pl.BlockSpec((tm, d), lambda i, f: (i, 0))] * n_out,
        out_shape=[jax.ShapeDtypeStruct((t, d), F32)] * n_out,
        scratch_shapes=[pltpu.VMEM((tm, d), BF16), pltpu.VMEM((tm, d), F32)],
        compiler_params=_cparams(("parallel", "arbitrary")),
        name="out_mlp",
    )(x, mix, mem_o, w_out_bf16, w_out_bf16, g.reshape(1, d), wu_bf16, wd_bf16, g_final.reshape(1, d))
    return outs if final_norm else (outs[0], None)


def _glu_kernel(y_ref, w_ref, o_ref):
    g = jax.nn.gelu(y_ref[...])
    gm = _dot(g.astype(BF16), w_ref[...])
    o_ref[...] = g * jax.nn.sigmoid(gm)


def glu(y, w_bf16, tm):
    t, n = y.shape
    return pl.pallas_call(
        _glu_kernel,
        grid=(t // tm,),
        in_specs=[pl.BlockSpec((tm, n), lambda i: (i, 0)), pl.BlockSpec((n, n), lambda i: (0, 0))],
        out_specs=pl.BlockSpec((tm, n), lambda i: (i, 0)),
        out_shape=jax.ShapeDtypeStruct((t, n), F32),
        compiler_params=_cparams(("parallel",)),
        name="glu",
    )(y, w_bf16)


def _mem_attn_kernel(q_ref, kt_ref, vt_ref, o_ref):
    q = q_ref[...] * ATTN_SCALE
    kt = kt_ref[...].astype(BF16)
    vt = vt_ref[...].astype(BF16)
    lane_head = lax.broadcasted_iota(jnp.int32, (1, MEM_WIDTH), 1) // HEAD_DIM
    out = jnp.zeros(q.shape, F32)
    for h in range(MEM_HEADS):
        m = lane_head == h
        s = _dot(jnp.where(m, q, 0.0).astype(BF16), kt)
        p = jnp.exp(s - jnp.max(s, axis=-1, keepdims=True))
        p = p / jnp.sum(p, axis=-1, keepdims=True)
        out = out + jnp.where(m, _dot_nt(p.astype(BF16), vt), 0.0)
    o_ref[...] = out


def mem_attn(qm, mkt, mvt, layer, n_seq, tm):
    t = qm.shape[0]
    seq_len = t // n_seq
    n_mem = mkt.shape[-1]
    nt = seq_len // tm
    kv_spec = pl.BlockSpec((None, None, MEM_WIDTH, n_mem), lambda b, i: (layer, b, 0, 0))
    return pl.pallas_call(
        _mem_attn_kernel,
        grid=(n_seq, nt),
        in_specs=[pl.BlockSpec((tm, MEM_WIDTH), lambda b, i: (b * nt + i, 0)), kv_spec, kv_spec],
        out_specs=pl.BlockSpec((tm, MEM_WIDTH), lambda b, i: (b * nt + i, 0)),
        out_shape=jax.ShapeDtypeStruct((t, MEM_WIDTH), F32),
        compiler_params=_cparams(("parallel", "parallel")),
        name="mem_attn",
    )(qm, mkt, mvt)


def _mem_attn_short_kernel(q_ref, kt_ref, vt_ref, o_ref, *, n_q):
    n_b = kt_ref.shape[0]
    rows = MEM_HEADS * n_q
    row = lax.broadcasted_iota(jnp.int32, (rows, 1), 0)
    head_mask = (lax.broadcasted_iota(jnp.int32, (1, MEM_WIDTH), 1) // HEAD_DIM) == (row // n_q)
    for b in range(n_b):
        q = q_ref[b * n_q:(b + 1) * n_q, :] * ATTN_SCALE
        qbd = jnp.where(head_mask, jnp.concatenate([q] * MEM_HEADS, axis=0), 0.0).astype(BF16)
        s = _dot(qbd, kt_ref[b].astype(BF16))
        p = jnp.exp(s - jnp.max(s, axis=-1, keepdims=True))
        p = p / jnp.sum(p, axis=-1, keepdims=True)
        a = jnp.where(head_mask, _dot_nt(p.astype(BF16), vt_ref[b].astype(BF16)), 0.0)
        out = a[0:n_q]
        for h in range(1, MEM_HEADS):
            out = out + a[h * n_q:(h + 1) * n_q]
        o_ref[b * n_q:(b + 1) * n_q, :] = out


def mem_attn_short(qm, mkt, mvt, layer, n_seq, n_b):
    t = qm.shape[0]
    n_q = t // n_seq
    n_mem = mkt.shape[-1]
    kv_spec = pl.BlockSpec((None, n_b, MEM_WIDTH, n_mem), lambda i: (layer, i, 0, 0))
    return pl.pallas_call(
        functools.partial(_mem_attn_short_kernel, n_q=n_q),
        grid=(n_seq // n_b,),
        in_specs=[pl.BlockSpec((n_b * n_q, MEM_WIDTH), lambda i: (i, 0)), kv_spec, kv_spec],
        out_specs=pl.BlockSpec((n_b * n_q, MEM_WIDTH), lambda i: (i, 0)),
        out_shape=jax.ShapeDtypeStruct((t, MEM_WIDTH), F32),
        compiler_params=_cparams(("parallel",)),
        name="mem_attn_short",
    )(qm, mkt, mvt)


def _ssm_prep_kernel(lr_ref, li_ref, ls_ref, lrc_ref, lic_ref, lsc_ref, btr_ref, bti_ref, ctr_ref, cti_ref,
                     m_ref, bc_ref, cc_ref, apr_ref, api_ref, hpr_ref, hpi_ref,
                     hc_scr, d_scr, pr_scr, pi_scr, bbr_scr, bbi_scr, *, lc, n_pow):
    s = pl.program_id(1)

    def discretise(lam_re, lam_im, log_step):
        a_re = jnp.minimum(lam_re, -1e-4)
        step = jnp.exp(log_step)
        mag = jnp.exp(step * a_re)
        return a_re, lam_im, mag * jnp.cos(step * lam_im), mag * jnp.sin(step * lam_im)

    @pl.when(s == 0)
    def _():
        a_re, a_im, lbr, lbi = discretise(lr_ref[...], li_ref[...], ls_ref[...])
        den = a_re * a_re + a_im * a_im
        nr, ni = _cmul(lbr - 1.0, lbi, a_re, -a_im)
        bbr, bbi = _cmul(nr / den, ni / den, btr_ref[...], bti_ref[...])
        bbr_scr[...] = bbr
        bbi_scr[...] = bbi
        pr, pi = jnp.ones_like(lbr), jnp.zeros_like(lbi)
        for j in range(lc + 1):
            pr_scr[j] = pr
            pi_scr[j] = pi
            if j == lc // 2:
                hpr_ref[...] = pr
                hpi_ref[...] = pi
            if j == lc:
                ar, ai = pr, pi
                for k in range(n_pow):
                    apr_ref[k:k + 1, :] = ar
                    api_ref[k:k + 1, :] = ai
                    ar, ai = _cmul(ar, ai, ar, ai)
            pr, pi = _cmul(pr, pi, lbr, lbi)
        _, _, lbrc, lbic = discretise(lrc_ref[...], lic_ref[...], lsc_ref[...])
        shape = ctr_ref.shape
        lbrc = jnp.broadcast_to(lbrc, shape)
        lbic = jnp.broadcast_to(lbic, shape)
        qr, qi = ctr_ref[...], cti_ref[...]
        bcat = jnp.concatenate([bbr, bbi], axis=1)
        for j in range(lc + 1):
            hj = jnp.concatenate([qr, -qi], axis=0)
            hc_scr[j] = hj
            if j < lc:
                d_scr[j] = jnp.dot(bcat, hj, precision=lax.Precision.HIGHEST, preferred_element_type=F32)
            qr, qi = _cmul(qr, qi, lbrc, lbic)

    pr, pi = pr_scr[s], pi_scr[s]
    gr, gi = _cmul(bbr_scr[...], bbi_scr[...], pr, pi)
    bc_ref[...] = jnp.concatenate([gr, gi], axis=1).astype(BF16)
    cc_ref[...] = hc_scr[s + 1].astype(BF16)
    for t in range(lc):
        blk = d_scr[jnp.maximum(t - s, 0)]
        m_ref[:, t * LANES:(t + 1) * LANES] = jnp.where(t >= s, blk, 0.0).astype(BF16)


def ssm_operators(lam_re, lam_im, log_step, b_re, b_im, c_re, c_im, lc, n_pow):
    n_g, n_p, n_c = b_re.shape
    n_slab = n_g // SLAB_GROUPS
    same = jnp.eye(SLAB_GROUPS, dtype=bool)

    def rows(x):
        return x.astype(F32).reshape(n_slab, 1, SLAB_STATE)

    def cols(x):
        return x.astype(F32).reshape(n_slab, SLAB_STATE, 1)

    def bt_blocks(b):
        bt = jnp.transpose(b.astype(F32), (0, 2, 1)).reshape(n_slab, SLAB_GROUPS, n_c, 1, n_p)
        bt = jnp.where(same[None, :, None, :, None], bt, 0.0)
        return bt.reshape(n_slab, LANES, SLAB_STATE)

    def ct_blocks(c):
        ct = jnp.transpose(c.astype(F32), (0, 2, 1)).reshape(n_slab, SLAB_GROUPS, n_p, 1, n_c)
        ct = jnp.where(same[None, :, None, :, None], ct, 0.0)
        return ct.reshape(n_slab, SLAB_STATE, LANES)

    step_gp = jnp.broadcast_to(log_step.astype(F32)[:, None], (n_g, n_p))
    row_spec = pl.BlockSpec((None, 1, SLAB_STATE), lambda g, s: (g, 0, 0))
    col_spec = pl.BlockSpec((None, SLAB_STATE, 1), lambda g, s: (g, 0, 0))
    bt_spec = pl.BlockSpec((None, LANES, SLAB_STATE), lambda g, s: (g, 0, 0))
    ct_spec = pl.BlockSpec((None, SLAB_STATE, LANES), lambda g, s: (g, 0, 0))
    pow_spec = pl.BlockSpec((None, n_pow, SLAB_STATE), lambda g, s: (g, 0, 0))
    kdim = lc * LANES
    half_shape = jax.ShapeDtypeStruct((n_slab, 1, SLAB_STATE), F32)
    return pl.pallas_call(
        functools.partial(_ssm_prep_kernel, lc=lc, n_pow=n_pow),
        grid=(n_slab, lc),
        in_specs=[row_spec] * 3 + [col_spec] * 3 + [bt_spec] * 2 + [ct_spec] * 2,
        out_specs=[
            pl.BlockSpec((None, LANES, kdim), lambda g, s: (g, s, 0)),
            pl.BlockSpec((None, LANES, 2 * SLAB_STATE), lambda g, s: (g, lc - 1 - s, 0)),
            pl.BlockSpec((None, 2 * SLAB_STATE, LANES), lambda g, s: (g, 0, s)),
            pow_spec, pow_spec, row_spec, row_spec,
        ],
        out_shape=[
            jax.ShapeDtypeStruct((n_slab, kdim, kdim), BF16),
            jax.ShapeDtypeStruct((n_slab, kdim, 2 * SLAB_STATE), BF16),
            jax.ShapeDtypeStruct((n_slab, 2 * SLAB_STATE, kdim), BF16),
            jax.ShapeDtypeStruct((n_slab, n_pow, SLAB_STATE), F32),
            jax.ShapeDtypeStruct((n_slab, n_pow, SLAB_STATE), F32),
            half_shape, half_shape,
        ],
        scratch_shapes=[
            pltpu.VMEM((lc + 1, 2 * SLAB_STATE, LANES), F32),
            pltpu.VMEM((lc, LANES, LANES), F32),
            pltpu.VMEM((lc + 1, 1, SLAB_STATE), F32),
            pltpu.VMEM((lc + 1, 1, SLAB_STATE), F32),
            pltpu.VMEM((LANES, SLAB_STATE), F32),
            pltpu.VMEM((LANES, SLAB_STATE), F32),
        ],
        compiler_params=_cparams(("parallel", "arbitrary")),
        name="ssm_prep",
    )(rows(lam_re), rows(lam_im), rows(step_gp), cols(lam_re), cols(lam_im), cols(step_gp),
      bt_blocks(b_re), bt_blocks(b_im), ct_blocks(c_re), ct_blocks(c_im))


def _ssm_kernel(u_ref, x0r_ref, x0i_ref, m_ref, bc_ref, cc_ref, d_ref, apr_ref, api_ref,
                y_ref, xr_ref, xi_ref, *, lc, nc):
    rows = u_ref.shape[0] // lc
    nb = rows // nc
    us = [u_ref[pl.ds(s, rows, stride=lc), :] for s in range(lc)]
    ucat = jnp.concatenate([u.astype(BF16) for u in us], axis=1)
    grp = MXU_DIM // LANES
    y = jnp.concatenate(
        [_dot(ucat[:, :(j + 1) * MXU_DIM], m_ref[:(j + 1) * MXU_DIM, j * MXU_DIM:(j + 1) * MXU_DIM])
         for j in range(lc // grp)], axis=1)
    v = _dot(ucat, bc_ref[...])
    vr, vi = v[:, :SLAB_STATE], v[:, SLAB_STATE:]
    x0r, x0i = x0r_ref[...], x0i_ref[...]
    inj_r, inj_i = _cmul(apr_ref[0:1, :], api_ref[0:1, :], x0r, x0i)
    if nc == 1:
        xr, xi = vr + inj_r, vi + inj_i
        pr, pi = x0r, x0i
        xr_ref[...] = xr
        xi_ref[...] = xi
    else:
        row = lax.broadcasted_iota(jnp.int32, (rows, 1), 0)
        chunk = row % nc

        def first_rows(x):
            out = jnp.zeros((rows, x.shape[1]), F32)
            for b in range(nb):
                out = jnp.where(row == b * nc, x[b:b + 1, :], out)
            return out

        vr = vr + first_rows(inj_r)
        vi = vi + first_rows(inj_i)
        for k in range(nc.bit_length() - 1):
            sh = 1 << k
            sr = jnp.where(chunk >= sh, pltpu.roll(vr, sh, 0), 0.0)
            si = jnp.where(chunk >= sh, pltpu.roll(vi, sh, 0), 0.0)
            dr, di = _cmul(apr_ref[k:k + 1, :], api_ref[k:k + 1, :], sr, si)
            vr, vi = vr + dr, vi + di
        for b in range(nb):
            last = (b + 1) * nc - 1
            xr_ref[b:b + 1, :] = vr[last:last + 1, :]
            xi_ref[b:b + 1, :] = vi[last:last + 1, :]
        pr = jnp.where(chunk >= 1, pltpu.roll(vr, 1, 0), first_rows(x0r))
        pi = jnp.where(chunk >= 1, pltpu.roll(vi, 1, 0), first_rows(x0i))
    prev = jnp.concatenate([pr.astype(BF16), pi.astype(BF16)], axis=1)
    y = y + _dot(prev, cc_ref[...])
    d = d_ref[...]
    for s in range(lc):
        y_ref[pl.ds(s, rows, stride=lc), :] = y[:, s * LANES:(s + 1) * LANES] + d * us[s]


def ssm_apply(u, x0r, x0i, ops, d_skip, lc, nc, nb):
    m, bc, cc, apr, api = ops
    bc_blk = bc.shape[1] // (lc * LANES) - 1
    t, width = u.shape
    n_slab = width // LANES
    tok = nb * nc * lc
    n_blk = t // tok
    kdim = lc * LANES
    n_pow = apr.shape[1]
    state_spec = pl.BlockSpec((None, None, nb, SLAB_STATE), lambda s, b: (s, b, 0, 0))
    pow_spec = pl.BlockSpec((None, n_pow, SLAB_STATE), lambda s, b: (s, 0, 0))
    state_shape = jax.ShapeDtypeStruct((n_slab, n_blk, nb, SLAB_STATE), F32)
    return pl.pallas_call(
        functools.partial(_ssm_kernel, lc=lc, nc=nc),
        grid=(n_slab, n_blk),
        in_specs=[
            pl.BlockSpec((tok, LANES), lambda s, b: (b, s)),
            state_spec, state_spec,
            pl.BlockSpec((None, kdim, kdim), lambda s, b: (s, 0, 0)),
            pl.BlockSpec((None, kdim, 2 * SLAB_STATE), lambda s, b: (s, bc_blk, 0)),
            pl.BlockSpec((None, 2 * SLAB_STATE, kdim), lambda s, b: (s, 0, 0)),
            pl.BlockSpec((None, 1, LANES), lambda s, b: (s, 0, 0)),
            pow_spec, pow_spec,
        ],
        out_specs=[pl.BlockSpec((tok, LANES), lambda s, b: (b, s)), state_spec, state_spec],
        out_shape=[jax.ShapeDtypeStruct((t, width), F32), state_shape, state_shape],
        compiler_params=_cparams(("arbitrary", "arbitrary")),
        name="ssm",
    )(u, x0r, x0i, m, bc, cc, d_skip.astype(F32).reshape(n_slab, 1, LANES), apr, api)


def _state_to_slabs(x, nb):
    n_seq, n_g, n_p = x.shape
    n_slab = n_g // SLAB_GROUPS
    x = x.reshape(n_seq, n_slab, SLAB_STATE)
    return jnp.transpose(x, (1, 0, 2)).reshape(n_slab, n_seq // nb, nb, SLAB_STATE)


def _slabs_to_state(x, n_p):
    n_slab, n_blk, nb, _ = x.shape
    x = jnp.transpose(x.reshape(n_slab, n_blk * nb, SLAB_STATE), (1, 0, 2))
    return x.reshape(n_blk * nb, n_slab * SLAB_GROUPS, n_p)


def _sb_weights(qh, kt_blk, bias, tri, carry, mask):
    n_keys = kt_blk.shape[1]
    n_tri = tri.shape[0]
    z = _dot(qh, kt_blk) + bias
    sp = jnp.maximum(z, 0.0) + jnp.log(1.0 + jnp.exp2(jnp.abs(z) * -LOG2_E))
    if mask is not None:
        sp = jnp.where(mask, sp, 0.0)
    spb = sp.astype(BF16)
    parts = []
    for c in reversed(range(n_keys // n_tri)):
        sl = slice(c * n_tri, (c + 1) * n_tri)
        parts.insert(0, _dot(spb[:, sl], tri) + carry)
        carry = carry + jnp.sum(sp[:, sl], axis=-1, keepdims=True)
    suffix = parts[0] if len(parts) == 1 else jnp.concatenate(parts, axis=1)
    w = jnp.exp2((z - sp - suffix) * LOG2_E)
    if mask is not None:
        w = jnp.where(mask, w, 0.0)
    return w.astype(BF16), carry


def _sb_prompt_kernel(bias_ref, q_ref, kt_ref, vt_ref, tri_ref, o_ref, *, tq, tk):
    pair = pl.program_id(1)
    i = pl.program_id(2)
    n_h = LANES // HEAD_DIM
    q = q_ref[...] * ATTN_SCALE
    tri = tri_ref[...]
    lane_head = lax.broadcasted_iota(jnp.int32, (1, LANES), 1) // HEAD_DIM
    qhs = [jnp.where(lane_head == hh, q, 0.0).astype(BF16) for hh in range(n_h)]
    biases = [bias_ref[pair * n_h + hh] for hh in range(n_h)]

    def load(start, n):
        return kt_ref[:, pl.ds(start, n)].astype(BF16), vt_ref[:, pl.ds(start, n)].astype(BF16)

    def diag_blocks(start):
        kt, vt = load(start, tq)
        half = tq // 2
        def causal(n_keys, q_offset):
            qpos = lax.broadcasted_iota(jnp.int32, (half, n_keys), 0)
            kpos = lax.broadcasted_iota(jnp.int32, (half, n_keys), 1)
            return kpos < qpos + q_offset

        mask_top = causal(half, 0)
        mask_bot = causal(tq, half)
        zero = jnp.zeros((half, 1), F32)
        res = []
        for hh in range(n_h):
            w_t, c_t = _sb_weights(qhs[hh][:half], kt[:, :half], biases[hh], tri, zero, mask_top)
            w_b, c_b = _sb_weights(qhs[hh][half:], kt, biases[hh], tri, zero, mask_bot)
            o = jnp.concatenate([_dot_nt(w_t, vt[:, :half]), _dot_nt(w_b, vt)], axis=0)
            lanes0 = jnp.zeros((half, LANES), F32)
            c = jnp.concatenate([c_t + lanes0, c_b + lanes0], axis=0)
            res.append((o, jnp.max(c, axis=1, keepdims=True)))
        return res

    res = diag_blocks(pl.multiple_of(i * tq, tq))

    def body(it, carry):
        kt, vt = load(pl.multiple_of(i * tq - (it + 1) * tk, tk), tk)
        res = []
        for hh in range(n_h):
            w, c = _sb_weights(qhs[hh], kt, biases[hh], tri, carry[hh][1], None)
            res.append((carry[hh][0] + _dot_nt(w, vt), c))
        return tuple(res)

    res = lax.fori_loop(0, i * (tq // tk), body, tuple(res))
    out = res[0][0]
    for hh in range(1, n_h):
        out = jnp.where(lane_head == hh, res[hh][0], out)
    o_ref[...] = out


def _tri(n):
    r = jnp.arange(n)
    return (r[:, None] > r[None, :]).astype(BF16)


def sb_prompt(q, kt, vt, layer, bias, tq, tk):
    t, width = q.shape
    _, n_seq, _, seq_len = kt.shape
    nq = seq_len // tq
    n_pair = width // LANES
    kv_spec = pl.BlockSpec((None, None, LANES, seq_len), lambda b, p, i: (layer, b, p, 0))
    return pl.pallas_call(
        functools.partial(_sb_prompt_kernel, tq=tq, tk=tk),
        grid=(n_seq, n_pair, nq),
        in_specs=[
            pl.BlockSpec(memory_space=pltpu.SMEM),
            pl.BlockSpec((tq, LANES), lambda b, p, i: (b * nq + i, p)),
            kv_spec, kv_spec,
            pl.BlockSpec((MXU_DIM, MXU_DIM), lambda b, p, i: (0, 0)),
        ],
        out_specs=pl.BlockSpec((tq, LANES), lambda b, p, i: (b * nq + i, p)),
        out_shape=jax.ShapeDtypeStruct((t, width), F32),
        compiler_params=_cparams(("parallel", "parallel", "arbitrary")),
        name="sb_prompt",
    )(bias.astype(F32), q, kt, vt, _tri(MXU_DIM))


def _sb_sample_kernel(pt_ref, q_ref, knt_ref, vnt_ref, bias_ref, tri_ref, *refs, n_q, n_pg):
    k_refs = refs[:n_pg]
    v_refs = refs[n_pg:2 * n_pg]
    o_ref = refs[2 * n_pg]
    acc_scr, carry_scr = refs[2 * n_pg + 1:]
    i = pl.program_id(1)
    width = q_ref.shape[1]
    n_heads = width // HEAD_DIM
    rows = bias_ref.shape[0]
    row = lax.broadcasted_iota(jnp.int32, (rows, 1), 0)
    head_mask = (lax.broadcasted_iota(jnp.int32, (1, width), 1) // HEAD_DIM) == (row // n_q)
    q = q_ref[...] * ATTN_SCALE
    qrep = jnp.concatenate([q] * (rows // n_q), axis=0)
    qbd = jnp.where(head_mask, qrep, 0.0).astype(BF16)
    bias = bias_ref[...]
    tri = tri_ref[...]

    @pl.when(i == 0)
    def _():
        kpos = lax.broadcasted_iota(jnp.int32, (rows, PAGE_SIZE), 1)
        mask = kpos < (row % n_q)
        w, c = _sb_weights(qbd, knt_ref[...].astype(BF16), bias, tri[:PAGE_SIZE, :PAGE_SIZE],
                           jnp.zeros((rows, 1), F32), mask)
        acc_scr[...] = _dot_nt(vnt_ref[...].astype(BF16), w)
        carry_scr[...] = c

    acc = acc_scr[...]
    carry = carry_scr[...]
    per_blk = MXU_DIM // PAGE_SIZE
    for j in range(n_pg // per_blk):
        pages = [j * per_blk + r for r in reversed(range(per_blk))]
        kt = jnp.concatenate([k_refs[p][...] for p in pages], axis=1).astype(BF16)
        vt = jnp.concatenate([v_refs[p][...] for p in pages], axis=1).astype(BF16)
        w, carry = _sb_weights(qbd, kt, bias, tri, carry, None)
        acc = acc + _dot_nt(vt, w)
    acc_scr[...] = acc
    carry_scr[...] = carry

    @pl.when(i == pl.num_programs(1) - 1)
    def _():
        a = jnp.where(head_mask, jnp.transpose(acc), 0.0)
        out = a[0:n_q]
        for h in range(1, n_heads):
            out = out + a[h * n_q:(h + 1) * n_q]
        o_ref[...] = out


def sb_sample(q, knt, vnt, cache_kt, cache_vt, layer, page_table, bias, n_q, n_pg=16):
    t, width = q.shape
    n_seq = t // n_q
    n_pages = page_table.shape[1]
    rows = (width // HEAD_DIM) * n_q
    rows_pad = -(-rows // LANES) * LANES
    bias_col = jnp.pad(jnp.repeat(bias.astype(F32), n_q), (0, rows_pad - rows)).reshape(rows_pad, 1)

    def page_spec(r):
        return pl.BlockSpec((None, None, width, PAGE_SIZE),
                            lambda b, i, pt: (layer, pt[b, n_pages - 1 - (i * n_pg + r)], 0, 0))

    new_spec = pl.BlockSpec((None, width, PAGE_SIZE), lambda b, i, pt: (b, 0, 0))
    grid_spec = pltpu.PrefetchScalarGridSpec(
        num_scalar_prefetch=1,
        grid=(n_seq, n_pages // n_pg),
        in_specs=[
            pl.BlockSpec((n_q, width), lambda b, i, pt: (b, 0)),
            new_spec, new_spec,
            pl.BlockSpec((rows_pad, 1), lambda b, i, pt: (0, 0)),
            pl.BlockSpec((MXU_DIM, MXU_DIM), lambda b, i, pt: (0, 0)),
        ] + [page_spec(r) for r in range(n_pg)] * 2,
        out_specs=pl.BlockSpec((n_q, width), lambda b, i, pt: (b, 0)),
        scratch_shapes=[pltpu.VMEM((width, rows_pad), F32), pltpu.VMEM((rows_pad, 1), F32)],
    )
    return pl.pallas_call(
        functools.partial(_sb_sample_kernel, n_q=n_q, n_pg=n_pg),
        grid_spec=grid_spec,
        out_shape=jax.ShapeDtypeStruct((t, width), F32),
        compiler_params=_cparams(("parallel", "arbitrary")),
        name="sb_sample",
    )(page_table, q, knt, vnt, bias_col, _tri(MXU_DIM), *([cache_kt] * n_pg), *([cache_vt] * n_pg))


def _pick_tile(t, pref):
    while t % pref:
        pref //= 2
    return pref


def _feature_major_pages(cache):
    n_l, pool, page, heads, d = cache.shape
    return jnp.transpose(cache, (0, 1, 3, 4, 2)).reshape(n_l, pool, heads * d, page)


def _new_tokens_feature_major(x, n_seq):
    n_q = x.shape[0] // n_seq
    xt = jnp.transpose(x.reshape(n_seq, n_q, x.shape[1]), (0, 2, 1))
    return jnp.pad(xt, ((0, 0), (0, 0), (0, PAGE_SIZE - n_q)))


def kernel(x_prompt, x_sample, mem_prompt, cache_sb_k, cache_sb_v, page_table, cache_mem_k, cache_mem_v, state_ssm_re, state_ssm_im, norm_mix, norm_mlp, norm_mem, norm_final, w_in_ssm, ssm_lambda_re, ssm_lambda_im, ssm_log_step, ssm_b_re, ssm_b_im, ssm_c_re, ssm_c_im, ssm_d, ssm_w_glu, w_in_sb, sb_bias, w_mem_kv, w_out, w_up, w_down):
    n_p, seq_p, d_model = x_prompt.shape
    n_s, seq_s, _ = x_sample.shape
    depth = w_out.shape[0]
    n_mem = mem_prompt.shape[1]
    tok_w = ssm_d.shape[1]
    n_groups, n_state = ssm_lambda_re.shape[1:]
    sb_heads = tok_w // HEAD_DIM
    lc_p = 16
    nc_p = seq_p // lc_p

    xp = x_prompt.reshape(n_p * seq_p, d_model)
    xs = x_sample.reshape(n_s * seq_s, d_model)
    tp, ts = xp.shape[0], xs.shape[0]
    tm_p = _pick_tile(seq_p, 512)
    tm_s = _pick_tile(ts, 256)

    w_in_ssm_b = w_in_ssm.astype(BF16)
    w_in_sb_b = w_in_sb.astype(BF16)
    wq_sb_b = jnp.concatenate([w_in_sb_b[:, :, :tok_w], w_in_sb_b[:, :, 3 * tok_w:]], axis=2)
    wkvt_sb_b = jnp.transpose(w_in_sb_b[:, :, tok_w:3 * tok_w], (0, 2, 1))
    w_out_b = w_out.astype(BF16)
    w_up_b = w_up.astype(BF16)
    w_down_b = w_down.astype(BF16)
    w_glu_b = ssm_w_glu.astype(BF16)
    cache_kt = _feature_major_pages(cache_sb_k)
    cache_vt = _feature_major_pages(cache_sb_v)
    n_lm, _, _, mem_heads, _ = cache_mem_k.shape
    mem_kt_s = jnp.transpose(cache_mem_k, (0, 1, 3, 4, 2)).reshape(n_lm, n_s, MEM_WIDTH, n_mem)
    mem_vt_s = jnp.transpose(cache_mem_v, (0, 1, 3, 4, 2)).reshape(n_lm, n_s, MEM_WIDTH, n_mem)

    mem_kt, mem_vt = mem_kv_all(mem_prompt, norm_mem, jnp.transpose(w_mem_kv, (0, 2, 1)).astype(BF16))

    ssm_re_p, ssm_im_p, ssm_re_s, ssm_im_s = [], [], [], []
    sb_k_s, sb_v_s = [], []
    kt_p = vt_p = None
    y_p = y_s = None
    for i in range(depth):
        j = i // 2
        if i % 2 == 0:
            prm = (ssm_lambda_re[j], ssm_lambda_im[j], ssm_log_step[j], ssm_b_re[j], ssm_b_im[j],
                   ssm_c_re[j], ssm_c_im[j])
            m_op, bc_op, cc_op, apr, api, hpr, hpi = ssm_operators(*prm, lc=lc_p, n_pow=nc_p.bit_length() - 1)
            ops_p = (m_op, bc_op, cc_op, apr, api)
            if 2 * seq_s == lc_p:
                ops_s = (m_op, bc_op, cc_op, hpr, hpi)
            else:
                ops_s = ssm_operators(*prm, lc=seq_s, n_pow=1)[:5]
            u_p, qm_p = norm_proj(xp, norm_mix[i], w_in_ssm_b[j], (tok_w, MEM_WIDTH), tm_p)
            u_s, qm_s = norm_proj(xs, norm_mix[i], w_in_ssm_b[j], (tok_w, MEM_WIDTH), tm_s)
            zero = jnp.zeros((n_p, n_groups, n_state), F32)
            nb_p = 2 if n_p % 2 == 0 else 1
            yv_p, fr_p, fi_p = ssm_apply(u_p, _state_to_slabs(zero, nb_p), _state_to_slabs(zero, nb_p),
                                         ops_p, ssm_d[j], lc_p, nc_p, nb_p)
            yv_s, fr_s, fi_s = ssm_apply(u_s, _state_to_slabs(state_ssm_re[j].astype(F32), n_s),
                                         _state_to_slabs(state_ssm_im[j].astype(F32), n_s),
                                         ops_s, ssm_d[j], seq_s, 1, n_s)
            mix_p = glu(yv_p, w_glu_b[j], tm_p)
            mix_s = glu(yv_s, w_glu_b[j], tm_s)
            ssm_re_p.append(_slabs_to_state(fr_p, n_state))
            ssm_im_p.append(_slabs_to_state(fi_p, n_state))
            ssm_re_s.append(_slabs_to_state(fr_s, n_state))
            ssm_im_s.append(_slabs_to_state(fi_s, n_state))
        else:
            q_p, qm_p, kt_p, vt_p = sb_proj(xp, norm_mix[i], wq_sb_b[j], wkvt_sb_b[j], n_p, tm_p, kt_p, vt_p)
            q_s, k_s, v_s, qm_s = norm_proj(xs, norm_mix[i], w_in_sb_b[j], (tok_w, tok_w, tok_w, MEM_WIDTH), tm_s)
            tq = _pick_tile(seq_p, 1024)
            mix_p = sb_prompt(q_p, kt_p, vt_p, j, sb_bias[j], tq, min(tq, 512))
            mix_s = sb_sample(q_s, _new_tokens_feature_major(k_s, n_s), _new_tokens_feature_major(v_s, n_s),
                              cache_kt, cache_vt, j, page_table, sb_bias[j], seq_s)
            sb_k_s.append(k_s.reshape(n_s, seq_s, sb_heads, HEAD_DIM))
            sb_v_s.append(v_s.reshape(n_s, seq_s, sb_heads, HEAD_DIM))
        mo_p = mem_attn(qm_p, mem_kt, mem_vt, i, n_p, tm_p)
        mo_s = mem_attn_short(qm_s, mem_kt_s, mem_vt_s, i, n_s, _pick_tile(n_s, 8))
        last = i == depth - 1
        xp, y_p = out_mlp(xp, mix_p, mo_p, i, w_out_b, norm_mlp[i], w_up_b, w_down_b, norm_final,
                          _pick_tile(tp, 1024), 1024, last)
        xs, y_s = out_mlp(xs, mix_s, mo_s, i, w_out_b, norm_mlp[i], w_up_b, w_down_b, norm_final,
                          tm_s, 1024, last)

    def token_major(xt):
        n_l, n_seq, _, n_pos = xt.shape
        return jnp.transpose(xt.reshape(n_l, n_seq, -1, HEAD_DIM, n_pos), (0, 1, 4, 2, 3))

    return (y_p.reshape(n_p, seq_p, d_model), y_s.reshape(n_s, seq_s, d_model),
            jnp.stack(ssm_re_p), jnp.stack(ssm_im_p), jnp.stack(ssm_re_s), jnp.stack(ssm_im_s),
            token_major(kt_p), token_major(vt_p),
            jnp.stack(sb_k_s), jnp.stack(sb_v_s),
            token_major(mem_kt), token_major(mem_vt))
```

```python
import functools
import math

import jax
import jax.numpy as jnp
from jax import lax
from jax.experimental import pallas as pl
from jax.experimental.pallas import tpu as pltpu

F32 = jnp.float32
BF16 = jnp.bfloat16

RMS_EPS = 1e-6
HEAD_DIM = 64
ATTN_SCALE = 1.0 / math.sqrt(HEAD_DIM)
LOG2_E = 1.0 / math.log(2.0)
MEM_HEADS = 4
MEM_WIDTH = MEM_HEADS * HEAD_DIM
SSM_GROUP_CH = 16
SSM_STATE = 64
PAGE_SIZE = 128

LANES = 128
MXU_DIM = 256
SLAB_GROUPS = LANES // SSM_GROUP_CH
SLAB_STATE = SLAB_GROUPS * SSM_STATE
VMEM_LIMIT = 56 * 1024 * 1024


def _cparams(sem):
    return pltpu.CompilerParams(dimension_semantics=sem, vmem_limit_bytes=VMEM_LIMIT)


def _rmsnorm(x, g):
    ms = jnp.mean(x * x, axis=-1, keepdims=True)
    return x * lax.rsqrt(ms + RMS_EPS) * g


def _dot(a, b):
    return jnp.dot(a, b, preferred_element_type=F32)


def _dot_nt(a, b):
    return lax.dot_general(a, b, (((1,), (1,)), ((), ())), preferred_element_type=F32)


def _cmul(ar, ai, br, bi):
    return ar * br - ai * bi, ar * bi + ai * br


def _norm_proj_kernel(x_ref, g_ref, w_ref, *o_refs, widths):
    h = _rmsnorm(x_ref[...], g_ref[...]).astype(BF16)
    r = _dot(h, w_ref[...])
    off = 0
    for o_ref, wd in zip(o_refs, widths):
        o_ref[...] = r[:, off:off + wd]
        off += wd


def norm_proj(x, g, w_bf16, widths, tm):
    t, d = x.shape
    n = w_bf16.shape[1]
    assert sum(widths) == n and t % tm == 0
    return pl.pallas_call(
        functools.partial(_norm_proj_kernel, widths=widths),
        grid=(t // tm,),
        in_specs=[
            pl.BlockSpec((tm, d), lambda i: (i, 0)),
            pl.BlockSpec((1, d), lambda i: (0, 0)),
            pl.BlockSpec((d, n), lambda i: (0, 0)),
        ],
        out_specs=[pl.BlockSpec((tm, wd), lambda i: (i, 0)) for wd in widths],
        out_shape=[jax.ShapeDtypeStruct((t, wd), F32) for wd in widths],
        compiler_params=_cparams(("parallel",)),
        name="norm_proj",
    )(x, g.reshape(1, d), w_bf16)


def _sb_proj_kernel(x_ref, g_ref, wq_ref, wkvt_ref, *refs, tok_w, n_prev):
    if n_prev:
        kt_prev_ref, vt_prev_ref, q_ref, qm_ref, kt_ref, vt_ref = refs
        kt_ref[:n_prev] = kt_prev_ref[...]
        vt_ref[:n_prev] = vt_prev_ref[...]
    else:
        q_ref, qm_ref, kt_ref, vt_ref = refs
    h = _rmsnorm(x_ref[...], g_ref[...]).astype(BF16)
    r = _dot(h, wq_ref[...])
    q_ref[...] = r[:, :tok_w]
    qm_ref[...] = r[:, tok_w:]
    kv = _dot_nt(wkvt_ref[...], h)
    kt_ref[n_prev] = kv[:tok_w]
    vt_ref[n_prev] = kv[tok_w:]


def sb_proj(x, g, wq_bf16, wkvt_bf16, n_seq, tm, kt_prev=None, vt_prev=None):
    t, d = x.shape
    seq_len = t // n_seq
    nt = seq_len // tm
    tok_w = wkvt_bf16.shape[0] // 2
    nq = wq_bf16.shape[1]
    n_prev = 0 if kt_prev is None else kt_prev.shape[0]
    prev_specs = [pl.BlockSpec((n_prev, None, tok_w, tm), lambda b, i: (0, b, 0, i))] * 2 if n_prev else []
    prev_args = [kt_prev, vt_prev] if n_prev else []
    stack_spec = pl.BlockSpec((n_prev + 1, None, tok_w, tm), lambda b, i: (0, b, 0, i))
    stack_shape = jax.ShapeDtypeStruct((n_prev + 1, n_seq, tok_w, seq_len), F32)
    return pl.pallas_call(
        functools.partial(_sb_proj_kernel, tok_w=tok_w, n_prev=n_prev),
        grid=(n_seq, nt),
        in_specs=[
            pl.BlockSpec((tm, d), lambda b, i: (b * nt + i, 0)),
            pl.BlockSpec((1, d), lambda b, i: (0, 0)),
            pl.BlockSpec((d, nq), lambda b, i: (0, 0)),
            pl.BlockSpec((2 * tok_w, d), lambda b, i: (0, 0)),
        ] + prev_specs,
        out_specs=[
            pl.BlockSpec((tm, tok_w), lambda b, i: (b * nt + i, 0)),
            pl.BlockSpec((tm, nq - tok_w), lambda b, i: (b * nt + i, 0)),
            stack_spec, stack_spec,
        ],
        out_shape=[
            jax.ShapeDtypeStruct((t, tok_w), F32),
            jax.ShapeDtypeStruct((t, nq - tok_w), F32),
            stack_shape, stack_shape,
        ],
        compiler_params=_cparams(("parallel", "parallel")),
        name="sb_proj",
    )(x, g.reshape(1, d), wq_bf16, wkvt_bf16, *prev_args)


def _mem_kv_kernel(x_ref, g_ref, wt_ref, k_ref, v_ref, h_scr):
    b = pl.program_id(1)

    @pl.when(pl.program_id(0) == 0)
    def _():
        h_scr[b] = _rmsnorm(x_ref[...], g_ref[...]).astype(BF16)

    r = _dot_nt(wt_ref[...], h_scr[b])
    k_ref[...] = r[:MEM_WIDTH]
    v_ref[...] = r[MEM_WIDTH:]


def mem_kv_all(mem, g, wt_bf16):
    n_seq, n_mem, d = mem.shape
    depth = wt_bf16.shape[0]
    return pl.pallas_call(
        _mem_kv_kernel,
        grid=(depth, n_seq),
        in_specs=[
            pl.BlockSpec((None, n_mem, d), lambda l, b: (b, 0, 0)),
            pl.BlockSpec((1, d), lambda l, b: (0, 0)),
            pl.BlockSpec((None, 2 * MEM_WIDTH, d), lambda l, b: (l, 0, 0)),
        ],
        out_specs=[pl.BlockSpec((None, None, MEM_WIDTH, n_mem), lambda l, b: (l, b, 0, 0))] * 2,
        out_shape=[jax.ShapeDtypeStruct((depth, n_seq, MEM_WIDTH, n_mem), F32)] * 2,
        scratch_shapes=[pltpu.VMEM((n_seq, n_mem, d), BF16)],
        compiler_params=_cparams(("arbitrary", "arbitrary")),
        name="mem_kv",
    )(mem, g.reshape(1, d), wt_bf16)


def _out_mlp_kernel(x_ref, mix_ref, mem_ref, w1_ref, w2_ref, g_ref, wu_ref, wd_ref, gf_ref, *refs,
                    nf, final_norm):
    if final_norm:
        o_ref, y_ref, h_scr, acc_scr = refs
    else:
        o_ref, h_scr, acc_scr = refs
    f = pl.program_id(1)

    @pl.when(f == 0)
    def _():
        proj = _dot(mix_ref[...].astype(BF16), w1_ref[...]) + _dot(mem_ref[...].astype(BF16), w2_ref[...])
        x1 = x_ref[...] + proj
        h_scr[...] = _rmsnorm(x1, g_ref[...]).astype(BF16)
        acc_scr[...] = x1

    a = _dot(h_scr[...], wu_ref[...])
    a = jnp.square(jnp.maximum(a, 0.0)).astype(BF16)
    acc_scr[...] += _dot(a, wd_ref[...])

    @pl.when(f == nf - 1)
    def _():
        x2 = acc_scr[...]
        o_ref[...] = x2
        if final_norm:
            y_ref[...] = _rmsnorm(x2, gf_ref[...])


def out_mlp(x, mix, mem_o, layer, w_out_bf16, g, wu_bf16, wd_bf16, g_final, tm, tf, final_norm):
    t, d = x.shape
    n1 = mix.shape[1]
    n2 = mem_o.shape[1]
    assert n1 % n2 == 0
    ff = wu_bf16.shape[2]
    nf = ff // tf
    n_out = 2 if final_norm else 1
    outs = pl.pallas_call(
        functools.partial(_out_mlp_kernel, nf=nf, final_norm=final_norm),
        grid=(t // tm, nf),
        in_specs=[
            pl.BlockSpec((tm, d), lambda i, f: (i, 0)),
            pl.BlockSpec((tm, n1), lambda i, f: (i, 0)),
            pl.BlockSpec((tm, n2), lambda i, f: (i, 0)),
            pl.BlockSpec((None, n1, d), lambda i, f: (layer, 0, 0)),
            pl.BlockSpec((None, n2, d), lambda i, f: (layer, n1 // n2, 0)),
            pl.BlockSpec((1, d), lambda i, f: (0, 0)),
            pl.BlockSpec((None, d, tf), lambda i, f: (layer, 0, f)),
            pl.BlockSpec((None, tf, d), lambda i, f: (layer, f, 0)),
            pl.BlockSpec((1, d), lambda i, f: (0, 0)),
        ],
        out_specs=[pl.BlockSpec((tm, d), lambda i, f: (i, 0))] * n_out,
        out_shape=[jax.ShapeDtypeStruct((t, d), F32)] * n_out,
        scratch_shapes=[pltpu.VMEM((tm, d), BF16), pltpu.VMEM((tm, d), F32)],
        compiler_params=_cparams(("parallel", "arbitrary")),
        name="out_mlp",
    )(x, mix, mem_o, w_out_bf16, w_out_bf16, g.reshape(1, d), wu_bf16, wd_bf16, g_final.reshape(1, d))
    return outs if final_norm else (outs[0], None)


def _glu_kernel(y_ref, w_ref, o_ref):
    g = jax.nn.gelu(y_ref[...])
    gm = _dot(g.astype(BF16), w_ref[...])
    o_ref[...] = g * jax.nn.sigmoid(gm)


def glu(y, w_bf16, tm):
    t, n = y.shape
    return pl.pallas_call(
        _glu_kernel,
        grid=(t // tm,),
        in_specs=[pl.BlockSpec((tm, n), lambda i: (i, 0)), pl.BlockSpec((n, n), lambda i: (0, 0))],
        out_specs=pl.BlockSpec((tm, n), lambda i: (i, 0)),
        out_shape=jax.ShapeDtypeStruct((t, n), F32),
        compiler_params=_cparams(("parallel",)),
        name="glu",
    )(y, w_bf16)


def _mem_attn_kernel(q_ref, kt_ref, vt_ref, o_ref):
    q = q_ref[...] * ATTN_SCALE
    kt = kt_ref[...].astype(BF16)
    vt = vt_ref[...].astype(BF16)
    lane_head = lax.broadcasted_iota(jnp.int32, (1, MEM_WIDTH), 1) // HEAD_DIM
    out = jnp.zeros(q.shape, F32)
    for h in range(MEM_HEADS):
        m = lane_head == h
        s = _dot(jnp.where(m, q, 0.0).astype(BF16), kt)
        p = jnp.exp(s - jnp.max(s, axis=-1, keepdims=True))
        p = p / jnp.sum(p, axis=-1, keepdims=True)
        out = out + jnp.where(m, _dot_nt(p.astype(BF16), vt), 0.0)
    o_ref[...] = out


def mem_attn(qm, mkt, mvt, layer, n_seq, tm):
    t = qm.shape[0]
    seq_len = t // n_seq
    n_mem = mkt.shape[-1]
    nt = seq_len // tm
    kv_spec = pl.BlockSpec((None, None, MEM_WIDTH, n_mem), lambda b, i: (layer, b, 0, 0))
    return pl.pallas_call(
        _mem_attn_kernel,
        grid=(n_seq, nt),
        in_specs=[pl.BlockSpec((tm, MEM_WIDTH), lambda b, i: (b * nt + i, 0)), kv_spec, kv_spec],
        out_specs=pl.BlockSpec((tm, MEM_WIDTH), lambda b, i: (b * nt + i, 0)),
        out_shape=jax.ShapeDtypeStruct((t, MEM_WIDTH), F32),
        compiler_params=_cparams(("parallel", "parallel")),
        name="mem_attn",
    )(qm, mkt, mvt)


def _mem_attn_short_kernel(q_ref, kt_ref, vt_ref, o_ref, *, n_q):
    n_b = kt_ref.shape[0]
    rows = MEM_HEADS * n_q
    row = lax.broadcasted_iota(jnp.int32, (rows, 1), 0)
    head_mask = (lax.broadcasted_iota(jnp.int32, (1, MEM_WIDTH), 1) // HEAD_DIM) == (row // n_q)
    for b in range(n_b):
        q = q_ref[b * n_q:(b + 1) * n_q, :] * ATTN_SCALE
        qbd = jnp.where(head_mask, jnp.concatenate([q] * MEM_HEADS, axis=0), 0.0).astype(BF16)
        s = _dot(qbd, kt_ref[b].astype(BF16))
        p = jnp.exp(s - jnp.max(s, axis=-1, keepdims=True))
        p = p / jnp.sum(p, axis=-1, keepdims=True)
        a = jnp.where(head_mask, _dot_nt(p.astype(BF16), vt_ref[b].astype(BF16)), 0.0)
        out = a[0:n_q]
        for h in range(1, MEM_HEADS):
            out = out + a[h * n_q:(h + 1) * n_q]
        o_ref[b * n_q:(b + 1) * n_q, :] = out


def mem_attn_short(qm, mkt, mvt, layer, n_seq, n_b):
    t = qm.shape[0]
    n_q = t // n_seq
    n_mem = mkt.shape[-1]
    kv_spec = pl.BlockSpec((None, n_b, MEM_WIDTH, n_mem), lambda i: (layer, i, 0, 0))
    return pl.pallas_call(
        functools.partial(_mem_attn_short_kernel, n_q=n_q),
        grid=(n_seq // n_b,),
        in_specs=[pl.BlockSpec((n_b * n_q, MEM_WIDTH), lambda i: (i, 0)), kv_spec, kv_spec],
        out_specs=pl.BlockSpec((n_b * n_q, MEM_WIDTH), lambda i: (i, 0)),
        out_shape=jax.ShapeDtypeStruct((t, MEM_WIDTH), F32),
        compiler_params=_cparams(("parallel",)),
        name="mem_attn_short",
    )(qm, mkt, mvt)


def _ssm_prep_kernel(lr_ref, li_ref, ls_ref, lrc_ref, lic_ref, lsc_ref, btr_ref, bti_ref, ctr_ref, cti_ref,
                     m_ref, bc_ref, cc_ref, apr_ref, api_ref, hpr_ref, hpi_ref,
                     hc_scr, d_scr, pr_scr, pi_scr, bbr_scr, bbi_scr, *, lc, n_pow):
    s = pl.program_id(1)

    def discretise(lam_re, lam_im, log_step):
        a_re = jnp.minimum(lam_re, -1e-4)
        step = jnp.exp(log_step)
        mag = jnp.exp(step * a_re)
        return a_re, lam_im, mag * jnp.cos(step * lam_im), mag * jnp.sin(step * lam_im)

    @pl.when(s == 0)
    def _():
        a_re, a_im, lbr, lbi = discretise(lr_ref[...], li_ref[...], ls_ref[...])
        den = a_re * a_re + a_im * a_im
        nr, ni = _cmul(lbr - 1.0, lbi, a_re, -a_im)
        bbr, bbi = _cmul(nr / den, ni / den, btr_ref[...], bti_ref[...])
        bbr_scr[...] = bbr
        bbi_scr[...] = bbi
        pr, pi = jnp.ones_like(lbr), jnp.zeros_like(lbi)
        for j in range(lc + 1):
            pr_scr[j] = pr
            pi_scr[j] = pi
            if j == lc // 2:
                hpr_ref[...] = pr
                hpi_ref[...] = pi
            if j == lc:
                ar, ai = pr, pi
                for k in range(n_pow):
                    apr_ref[k:k + 1, :] = ar
                    api_ref[k:k + 1, :] = ai
                    ar, ai = _cmul(ar, ai, ar, ai)
            pr, pi = _cmul(pr, pi, lbr, lbi)
        _, _, lbrc, lbic = discretise(lrc_ref[...], lic_ref[...], lsc_ref[...])
        shape = ctr_ref.shape
        lbrc = jnp.broadcast_to(lbrc, shape)
        lbic = jnp.broadcast_to(lbic, shape)
        qr, qi = ctr_ref[...], cti_ref[...]
        bcat = jnp.concatenate([bbr, bbi], axis=1)
        for j in range(lc + 1):
            hj = jnp.concatenate([qr, -qi], axis=0)
            hc_scr[j] = hj
            if j < lc:
                d_scr[j] = jnp.dot(bcat, hj, precision=lax.Precision.HIGHEST, preferred_element_type=F32)
            qr, qi = _cmul(qr, qi, lbrc, lbic)

    pr, pi = pr_scr[s], pi_scr[s]
    gr, gi = _cmul(bbr_scr[...], bbi_scr[...], pr, pi)
    bc_ref[...] = jnp.concatenate([gr, gi], axis=1).astype(BF16)
    cc_ref[...] = hc_scr[s + 1].astype(BF16)
    for t in range(lc):
        blk = d_scr[jnp.maximum(t - s, 0)]
        m_ref[:, t * LANES:(t + 1) * LANES] = jnp.where(t >= s, blk, 0.0).astype(BF16)


def ssm_operators(lam_re, lam_im, log_step, b_re, b_im, c_re, c_im, lc, n_pow):
    n_g, n_p, n_c = b_re.shape
    n_slab = n_g // SLAB_GROUPS
    same = jnp.eye(SLAB_GROUPS, dtype=bool)

    def rows(x):
        return x.astype(F32).reshape(n_slab, 1, SLAB_STATE)

    def cols(x):
        return x.astype(F32).reshape(n_slab, SLAB_STATE, 1)

    def bt_blocks(b):
        bt = jnp.transpose(b.astype(F32), (0, 2, 1)).reshape(n_slab, SLAB_GROUPS, n_c, 1, n_p)
        bt = jnp.where(same[None, :, None, :, None], bt, 0.0)
        return bt.reshape(n_slab, LANES, SLAB_STATE)

    def ct_blocks(c):
        ct = jnp.transpose(c.astype(F32), (0, 2, 1)).reshape(n_slab, SLAB_GROUPS, n_p, 1, n_c)
        ct = jnp.where(same[None, :, None, :, None], ct, 0.0)
        return ct.reshape(n_slab, SLAB_STATE, LANES)

    step_gp = jnp.broadcast_to(log_step.astype(F32)[:, None], (n_g, n_p))
    row_spec = pl.BlockSpec((None, 1, SLAB_STATE), lambda g, s: (g, 0, 0))
    col_spec = pl.BlockSpec((None, SLAB_STATE, 1), lambda g, s: (g, 0, 0))
    bt_spec = pl.BlockSpec((None, LANES, SLAB_STATE), lambda g, s: (g, 0, 0))
    ct_spec = pl.BlockSpec((None, SLAB_STATE, LANES), lambda g, s: (g, 0, 0))
    pow_spec = pl.BlockSpec((None, n_pow, SLAB_STATE), lambda g, s: (g, 0, 0))
    kdim = lc * LANES
    half_shape = jax.ShapeDtypeStruct((n_slab, 1, SLAB_STATE), F32)
    return pl.pallas_call(
        functools.partial(_ssm_prep_kernel, lc=lc, n_pow=n_pow),
        grid=(n_slab, lc),
        in_specs=[row_spec] * 3 + [col_spec] * 3 + [bt_spec] * 2 + [ct_spec] * 2,
        out_specs=[
            pl.BlockSpec((None, LANES, kdim), lambda g, s: (g, s, 0)),
            pl.BlockSpec((None, LANES, 2 * SLAB_STATE), lambda g, s: (g, lc - 1 - s, 0)),
            pl.BlockSpec((None, 2 * SLAB_STATE, LANES), lambda g, s: (g, 0, s)),
            pow_spec, pow_spec, row_spec, row_spec,
        ],
        out_shape=[
            jax.ShapeDtypeStruct((n_slab, kdim, kdim), BF16),
            jax.ShapeDtypeStruct((n_slab, kdim, 2 * SLAB_STATE), BF16),
            jax.ShapeDtypeStruct((n_slab, 2 * SLAB_STATE, kdim), BF16),
            jax.ShapeDtypeStruct((n_slab, n_pow, SLAB_STATE), F32),
            jax.ShapeDtypeStruct((n_slab, n_pow, SLAB_STATE), F32),
            half_shape, half_shape,
        ],
        scratch_shapes=[
            pltpu.VMEM((lc + 1, 2 * SLAB_STATE, LANES), F32),
            pltpu.VMEM((lc, LANES, LANES), F32),
            pltpu.VMEM((lc + 1, 1, SLAB_STATE), F32),
            pltpu.VMEM((lc + 1, 1, SLAB_STATE), F32),
            pltpu.VMEM((LANES, SLAB_STATE), F32),
            pltpu.VMEM((LANES, SLAB_STATE), F32),
        ],
        compiler_params=_cparams(("parallel", "arbitrary")),
        name="ssm_prep",
    )(rows(lam_re), rows(lam_im), rows(step_gp), cols(lam_re), cols(lam_im), cols(step_gp),
      bt_blocks(b_re), bt_blocks(b_im), ct_blocks(c_re), ct_blocks(c_im))


def _ssm_kernel(u_ref, x0r_ref, x0i_ref, m_ref, bc_ref, cc_ref, d_ref, apr_ref, api_ref,
                y_ref, xr_ref, xi_ref, *, lc, nc):
    rows = u_ref.shape[0] // lc
    nb = rows // nc
    us = [u_ref[pl.ds(s, rows, stride=lc), :] for s in range(lc)]
    ucat = jnp.concatenate([u.astype(BF16) for u in us], axis=1)
    grp = MXU_DIM // LANES
    y = jnp.concatenate(
        [_dot(ucat[:, :(j + 1) * MXU_DIM], m_ref[:(j + 1) * MXU_DIM, j * MXU_DIM:(j + 1) * MXU_DIM])
         for j in range(lc // grp)], axis=1)
    v = _dot(ucat, bc_ref[...])
    vr, vi = v[:, :SLAB_STATE], v[:, SLAB_STATE:]
    x0r, x0i = x0r_ref[...], x0i_ref[...]
    inj_r, inj_i = _cmul(apr_ref[0:1, :], api_ref[0:1, :], x0r, x0i)
    if nc == 1:
        xr, xi = vr + inj_r, vi + inj_i
        pr, pi = x0r, x0i
        xr_ref[...] = xr
        xi_ref[...] = xi
    else:
        row = lax.broadcasted_iota(jnp.int32, (rows, 1), 0)
        chunk = row % nc

        def first_rows(x):
            out = jnp.zeros((rows, x.shape[1]), F32)
            for b in range(nb):
                out = jnp.where(row == b * nc, x[b:b + 1, :], out)
            return out

        vr = vr + first_rows(inj_r)
        vi = vi + first_rows(inj_i)
        for k in range(nc.bit_length() - 1):
            sh = 1 << k
            sr = jnp.where(chunk >= sh, pltpu.roll(vr, sh, 0), 0.0)
            si = jnp.where(chunk >= sh, pltpu.roll(vi, sh, 0), 0.0)
            dr, di = _cmul(apr_ref[k:k + 1, :], api_ref[k:k + 1, :], sr, si)
            vr, vi = vr + dr, vi + di
        for b in range(nb):
            last = (b + 1) * nc - 1
            xr_ref[b:b + 1, :] = vr[last:last + 1, :]
            xi_ref[b:b + 1, :] = vi[last:last + 1, :]
        pr = jnp.where(chunk >= 1, pltpu.roll(vr, 1, 0), first_rows(x0r))
        pi = jnp.where(chunk >= 1, pltpu.roll(vi, 1, 0), first_rows(x0i))
    prev = jnp.concatenate([pr.astype(BF16), pi.astype(BF16)], axis=1)
    y = y + _dot(prev, cc_ref[...])
    d = d_ref[...]
    for s in range(lc):
        y_ref[pl.ds(s, rows, stride=lc), :] = y[:, s * LANES:(s + 1) * LANES] + d * us[s]


def ssm_apply(u, x0r, x0i, ops, d_skip, lc, nc, nb):
    m, bc, cc, apr, api = ops
    bc_blk = bc.shape[1] // (lc * LANES) - 1
    t, width = u.shape
    n_slab = width // LANES
    tok = nb * nc * lc
    n_blk = t // tok
    kdim = lc * LANES
    n_pow = apr.shape[1]
    state_spec = pl.BlockSpec((None, None, nb, SLAB_STATE), lambda s, b: (s, b, 0, 0))
    pow_spec = pl.BlockSpec((None, n_pow, SLAB_STATE), lambda s, b: (s, 0, 0))
    state_shape = jax.ShapeDtypeStruct((n_slab, n_blk, nb, SLAB_STATE), F32)
    return pl.pallas_call(
        functools.partial(_ssm_kernel, lc=lc, nc=nc),
        grid=(n_slab, n_blk),
        in_specs=[
            pl.BlockSpec((tok, LANES), lambda s, b: (b, s)),
            state_spec, state_spec,
            pl.BlockSpec((None, kdim, kdim), lambda s, b: (s, 0, 0)),
            pl.BlockSpec((None, kdim, 2 * SLAB_STATE), lambda s, b: (s, bc_blk, 0)),
            pl.BlockSpec((None, 2 * SLAB_STATE, kdim), lambda s, b: (s, 0, 0)),
            pl.BlockSpec((None, 1, LANES), lambda s, b: (s, 0, 0)),
            pow_spec, pow_spec,
        ],
        out_specs=[pl.BlockSpec((tok, LANES), lambda s, b: (b, s)), state_spec, state_spec],
        out_shape=[jax.ShapeDtypeStruct((t, width), F32), state_shape, state_shape],
        compiler_params=_cparams(("arbitrary", "arbitrary")),
        name="ssm",
    )(u, x0r, x0i, m, bc, cc, d_skip.astype(F32).reshape(n_slab, 1, LANES), apr, api)


def _state_to_slabs(x, nb):
    n_seq, n_g, n_p = x.shape
    n_slab = n_g // SLAB_GROUPS
    x = x.reshape(n_seq, n_slab, SLAB_STATE)
    return jnp.transpose(x, (1, 0, 2)).reshape(n_slab, n_seq // nb, nb, SLAB_STATE)


def _slabs_to_state(x, n_p):
    n_slab, n_blk, nb, _ = x.shape
    x = jnp.transpose(x.reshape(n_slab, n_blk * nb, SLAB_STATE), (1, 0, 2))
    return x.reshape(n_blk * nb, n_slab * SLAB_GROUPS, n_p)


def _sb_weights(qh, kt_blk, bias, tri, carry, mask):
    n_keys = kt_blk.shape[1]
    n_tri = tri.shape[0]
    z = _dot(qh, kt_blk) + bias
    sp = jnp.maximum(z, 0.0) + jnp.log(1.0 + jnp.exp2(jnp.abs(z) * -LOG2_E))
    if mask is not None:
        sp = jnp.where(mask, sp, 0.0)
    spb = sp.astype(BF16)
    parts = []
    for c in reversed(range(n_keys // n_tri)):
        sl = slice(c * n_tri, (c + 1) * n_tri)
        parts.insert(0, _dot(spb[:, sl], tri) + carry)
        carry = carry + jnp.sum(sp[:, sl], axis=-1, keepdims=True)
    suffix = parts[0] if len(parts) == 1 else jnp.concatenate(parts, axis=1)
    w = jnp.exp2((z - sp - suffix) * LOG2_E)
    if mask is not None:
        w = jnp.where(mask, w, 0.0)
    return w.astype(BF16), carry


def _sb_prompt_kernel(bias_ref, q_ref, kt_ref, vt_ref, tri_ref, o_ref, *, tq, tk):
    pair = pl.program_id(1)
    i = pl.program_id(2)
    n_h = LANES // HEAD_DIM
    q = q_ref[...] * ATTN_SCALE
    tri = tri_ref[...]
    lane_head = lax.broadcasted_iota(jnp.int32, (1, LANES), 1) // HEAD_DIM
    qhs = [jnp.where(lane_head == hh, q, 0.0).astype(BF16) for hh in range(n_h)]
    biases = [bias_ref[pair * n_h + hh] for hh in range(n_h)]

    def load(start, n):
        return kt_ref[:, pl.ds(start, n)].astype(BF16), vt_ref[:, pl.ds(start, n)].astype(BF16)

    def diag_blocks(start):
        kt, vt = load(start, tq)
        half = tq // 2
        def causal(n_keys, q_offset):
            qpos = lax.broadcasted_iota(jnp.int32, (half, n_keys), 0)
            kpos = lax.broadcasted_iota(jnp.int32, (half, n_keys), 1)
            return kpos < qpos + q_offset

        mask_top = causal(half, 0)
        mask_bot = causal(tq, half)
        zero = jnp.zeros((half, 1), F32)
        res = []
        for hh in range(n_h):
            w_t, c_t = _sb_weights(qhs[hh][:half], kt[:, :half], biases[hh], tri, zero, mask_top)
            w_b, c_b = _sb_weights(qhs[hh][half:], kt, biases[hh], tri, zero, mask_bot)
            o = jnp.concatenate([_dot_nt(w_t, vt[:, :half]), _dot_nt(w_b, vt)], axis=0)
            lanes0 = jnp.zeros((half, LANES), F32)
            c = jnp.concatenate([c_t + lanes0, c_b + lanes0], axis=0)
            res.append((o, jnp.max(c, axis=1, keepdims=True)))
        return res

    res = diag_blocks(pl.multiple_of(i * tq, tq))

    def body(it, carry):
        kt, vt = load(pl.multiple_of(i * tq - (it + 1) * tk, tk), tk)
        res = []
        for hh in range(n_h):
            w, c = _sb_weights(qhs[hh], kt, biases[hh], tri, carry[hh][1], None)
            res.append((carry[hh][0] + _dot_nt(w, vt), c))
        return tuple(res)

    res = lax.fori_loop(0, i * (tq // tk), body, tuple(res))
    out = res[0][0]
    for hh in range(1, n_h):
        out = jnp.where(lane_head == hh, res[hh][0], out)
    o_ref[...] = out


def _tri(n):
    r = jnp.arange(n)
    return (r[:, None] > r[None, :]).astype(BF16)


def sb_prompt(q, kt, vt, layer, bias, tq, tk):
    t, width = q.shape
    _, n_seq, _, seq_len = kt.shape
    nq = seq_len // tq
    n_pair = width // LANES
    kv_spec = pl.BlockSpec((None, None, LANES, seq_len), lambda b, p, i: (layer, b, p, 0))
    return pl.pallas_call(
        functools.partial(_sb_prompt_kernel, tq=tq, tk=tk),
        grid=(n_seq, n_pair, nq),
        in_specs=[
            pl.BlockSpec(memory_space=pltpu.SMEM),
            pl.BlockSpec((tq, LANES), lambda b, p, i: (b * nq + i, p)),
            kv_spec, kv_spec,
            pl.BlockSpec((MXU_DIM, MXU_DIM), lambda b, p, i: (0, 0)),
        ],
        out_specs=pl.BlockSpec((tq, LANES), lambda b, p, i: (b * nq + i, p)),
        out_shape=jax.ShapeDtypeStruct((t, width), F32),
        compiler_params=_cparams(("parallel", "parallel", "arbitrary")),
        name="sb_prompt",
    )(bias.astype(F32), q, kt, vt, _tri(MXU_DIM))


def _sb_sample_kernel(pt_ref, q_ref, knt_ref, vnt_ref, bias_ref, tri_ref, *refs, n_q, n_pg):
    k_refs = refs[:n_pg]
    v_refs = refs[n_pg:2 * n_pg]
    o_ref = refs[2 * n_pg]
    acc_scr, carry_scr = refs[2 * n_pg + 1:]
    i = pl.program_id(1)
    width = q_ref.shape[1]
    n_heads = width // HEAD_DIM
    rows = bias_ref.shape[0]
    row = lax.broadcasted_iota(jnp.int32, (rows, 1), 0)
    head_mask = (lax.broadcasted_iota(jnp.int32, (1, width), 1) // HEAD_DIM) == (row // n_q)
    q = q_ref[...] * ATTN_SCALE
    qrep = jnp.concatenate([q] * (rows // n_q), axis=0)
    qbd = jnp.where(head_mask, qrep, 0.0).astype(BF16)
    bias = bias_ref[...]
    tri = tri_ref[...]

    @pl.when(i == 0)
    def _():
        kpos = lax.broadcasted_iota(jnp.int32, (rows, PAGE_SIZE), 1)
        mask = kpos < (row % n_q)
        w, c = _sb_weights(qbd, knt_ref[...].astype(BF16), bias, tri[:PAGE_SIZE, :PAGE_SIZE],
                           jnp.zeros((rows, 1), F32), mask)
        acc_scr[...] = _dot_nt(vnt_ref[...].astype(BF16), w)
        carry_scr[...] = c

    acc = acc_scr[...]
    carry = carry_scr[...]
    per_blk = MXU_DIM // PAGE_SIZE
    for j in range(n_pg // per_blk):
        pages = [j * per_blk + r for r in reversed(range(per_blk))]
        kt = jnp.concatenate([k_refs[p][...] for p in pages], axis=1).astype(BF16)
        vt = jnp.concatenate([v_refs[p][...] for p in pages], axis=1).astype(BF16)
        w, carry = _sb_weights(qbd, kt, bias, tri, carry, None)
        acc = acc + _dot_nt(vt, w)
    acc_scr[...] = acc
    carry_scr[...] = carry

    @pl.when(i == pl.num_programs(1) - 1)
    def _():
        a = jnp.where(head_mask, jnp.transpose(acc), 0.0)
        out = a[0:n_q]
        for h in range(1, n_heads):
            out = out + a[h * n_q:(h + 1) * n_q]
        o_ref[...] = out


def sb_sample(q, knt, vnt, cache_kt, cache_vt, layer, page_table, bias, n_q, n_pg=16):
    t, width = q.shape
    n_seq = t // n_q
    n_pages = page_table.shape[1]
    rows = (width // HEAD_DIM) * n_q
    rows_pad = -(-rows // LANES) * LANES
    bias_col = jnp.pad(jnp.repeat(bias.astype(F32), n_q), (0, rows_pad - rows)).reshape(rows_pad, 1)

    def page_spec(r):
        return pl.BlockSpec((None, None, width, PAGE_SIZE),
                            lambda b, i, pt: (layer, pt[b, n_pages - 1 - (i * n_pg + r)], 0, 0))

    new_spec = pl.BlockSpec((None, width, PAGE_SIZE), lambda b, i, pt: (b, 0, 0))
    grid_spec = pltpu.PrefetchScalarGridSpec(
        num_scalar_prefetch=1,
        grid=(n_seq, n_pages // n_pg),
        in_specs=[
            pl.BlockSpec((n_q, width), lambda b, i, pt: (b, 0)),
            new_spec, new_spec,
            pl.BlockSpec((rows_pad, 1), lambda b, i, pt: (0, 0)),
            pl.BlockSpec((MXU_DIM, MXU_DIM), lambda b, i, pt: (0, 0)),
        ] + [page_spec(r) for r in range(n_pg)] * 2,
        out_specs=pl.BlockSpec((n_q, width), lambda b, i, pt: (b, 0)),
        scratch_shapes=[pltpu.VMEM((width, rows_pad), F32), pltpu.VMEM((rows_pad, 1), F32)],
    )
    return pl.pallas_call(
        functools.partial(_sb_sample_kernel, n_q=n_q, n_pg=n_pg),
        grid_spec=grid_spec,
        out_shape=jax.ShapeDtypeStruct((t, width), F32),
        compiler_params=_cparams(("parallel", "arbitrary")),
        name="sb_sample",
    )(page_table, q, knt, vnt, bias_col, _tri(MXU_DIM), *([cache_kt] * n_pg), *([cache_vt] * n_pg))


def _pick_tile(t, pref):
    while t % pref:
        pref //= 2
    return pref


def _feature_major_pages(cache):
    n_l, pool, page, heads, d = cache.shape
    return jnp.transpose(cache, (0, 1, 3, 4, 2)).reshape(n_l, pool, heads * d, page)


def _new_tokens_feature_major(x, n_seq):
    n_q = x.shape[0] // n_seq
    xt = jnp.transpose(x.reshape(n_seq, n_q, x.shape[1]), (0, 2, 1))
    return jnp.pad(xt, ((0, 0), (0, 0), (0, PAGE_SIZE - n_q)))


def kernel(x_prompt, x_sample, mem_prompt, cache_sb_k, cache_sb_v, page_table, cache_mem_k, cache_mem_v, state_ssm_re, state_ssm_im, norm_mix, norm_mlp, norm_mem, norm_final, w_in_ssm, ssm_lambda_re, ssm_lambda_im, ssm_log_step, ssm_b_re, ssm_b_im, ssm_c_re, ssm_c_im, ssm_d, ssm_w_glu, w_in_sb, sb_bias, w_mem_kv, w_out, w_up, w_down):
    n_p, seq_p, d_model = x_prompt.shape
    n_s, seq_s, _ = x_sample.shape
    depth = w_out.shape[0]
    n_mem = mem_prompt.shape[1]
    tok_w = ssm_d.shape[1]
    n_groups, n_state = ssm_lambda_re.shape[1:]
    sb_heads = tok_w // HEAD_DIM
    lc_p = 16
    nc_p = seq_p // lc_p

    xp = x_prompt.reshape(n_p * seq_p, d_model)
    xs = x_sample.reshape(n_s * seq_s, d_model)
    tp, ts = xp.shape[0], xs.shape[0]
    tm_p = _pick_tile(seq_p, 512)
    tm_wide = _pick_tile(seq_p, 1024)
    tm_s = _pick_tile(ts, 256)

    w_in_ssm_b = w_in_ssm.astype(BF16)
    w_in_sb_b = w_in_sb.astype(BF16)
    wq_sb_b = jnp.concatenate([w_in_sb_b[:, :, :tok_w], w_in_sb_b[:, :, 3 * tok_w:]], axis=2)
    wkvt_sb_b = jnp.transpose(w_in_sb_b[:, :, tok_w:3 * tok_w], (0, 2, 1))
    w_out_b = w_out.astype(BF16)
    w_up_b = w_up.astype(BF16)
    w_down_b = w_down.astype(BF16)
    w_glu_b = ssm_w_glu.astype(BF16)
    cache_kt = _feature_major_pages(cache_sb_k)
    cache_vt = _feature_major_pages(cache_sb_v)
    n_lm, _, _, mem_heads, _ = cache_mem_k.shape
    mem_kt_s = jnp.transpose(cache_mem_k, (0, 1, 3, 4, 2)).reshape(n_lm, n_s, MEM_WIDTH, n_mem)
    mem_vt_s = jnp.transpose(cache_mem_v, (0, 1, 3, 4, 2)).reshape(n_lm, n_s, MEM_WIDTH, n_mem)

    mem_kt, mem_vt = mem_kv_all(mem_prompt, norm_mem, jnp.transpose(w_mem_kv, (0, 2, 1)).astype(BF16))

    ssm_re_p, ssm_im_p, ssm_re_s, ssm_im_s = [], [], [], []
    sb_k_s, sb_v_s = [], []
    kt_p = vt_p = None
    y_p = y_s = None
    for i in range(depth):
        j = i // 2
        if i % 2 == 0:
            prm = (ssm_lambda_re[j], ssm_lambda_im[j], ssm_log_step[j], ssm_b_re[j], ssm_b_im[j],
                   ssm_c_re[j], ssm_c_im[j])
            m_op, bc_op, cc_op, apr, api, hpr, hpi = ssm_operators(*prm, lc=lc_p, n_pow=nc_p.bit_length() - 1)
            ops_p = (m_op, bc_op, cc_op, apr, api)
            if 2 * seq_s == lc_p:
                ops_s = (m_op, bc_op, cc_op, hpr, hpi)
            else:
                ops_s = ssm_operators(*prm, lc=seq_s, n_pow=1)[:5]
            u_p, qm_p = norm_proj(xp, norm_mix[i], w_in_ssm_b[j], (tok_w, MEM_WIDTH), tm_wide)
            u_s, qm_s = norm_proj(xs, norm_mix[i], w_in_ssm_b[j], (tok_w, MEM_WIDTH), tm_s)
            zero = jnp.zeros((n_p, n_groups, n_state), F32)
            nb_p = 2 if n_p % 2 == 0 else 1
            yv_p, fr_p, fi_p = ssm_apply(u_p, _state_to_slabs(zero, nb_p), _state_to_slabs(zero, nb_p),
                                         ops_p, ssm_d[j], lc_p, nc_p, nb_p)
            yv_s, fr_s, fi_s = ssm_apply(u_s, _state_to_slabs(state_ssm_re[j].astype(F32), n_s),
                                         _state_to_slabs(state_ssm_im[j].astype(F32), n_s),
                                         ops_s, ssm_d[j], seq_s, 1, n_s)
            mix_p = glu(yv_p, w_glu_b[j], tm_wide)
            mix_s = glu(yv_s, w_glu_b[j], tm_s)
            ssm_re_p.append(_slabs_to_state(fr_p, n_state))
            ssm_im_p.append(_slabs_to_state(fi_p, n_state))
            ssm_re_s.append(_slabs_to_state(fr_s, n_state))
            ssm_im_s.append(_slabs_to_state(fi_s, n_state))
        else:
            q_p, qm_p, kt_p, vt_p = sb_proj(xp, norm_mix[i], wq_sb_b[j], wkvt_sb_b[j], n_p, tm_p, kt_p, vt_p)
            q_s, k_s, v_s, qm_s = norm_proj(xs, norm_mix[i], w_in_sb_b[j], (tok_w, tok_w, tok_w, MEM_WIDTH), tm_s)
            tq = _pick_tile(seq_p, 1024)
            mix_p = sb_prompt(q_p, kt_p, vt_p, j, sb_bias[j], tq, min(tq, 512))
            mix_s = sb_sample(q_s, _new_tokens_feature_major(k_s, n_s), _new_tokens_feature_major(v_s, n_s),
                              cache_kt, cache_vt, j, page_table, sb_bias[j], seq_s)
            sb_k_s.append(k_s.reshape(n_s, seq_s, sb_heads, HEAD_DIM))
            sb_v_s.append(v_s.reshape(n_s, seq_s, sb_heads, HEAD_DIM))
        mo_p = mem_attn(qm_p, mem_kt, mem_vt, i, n_p, tm_wide)
        mo_s = mem_attn_short(qm_s, mem_kt_s, mem_vt_s, i, n_s, _pick_tile(n_s, 8))
        last = i == depth - 1
        xp, y_p = out_mlp(xp, mix_p, mo_p, i, w_out_b, norm_mlp[i], w_up_b, w_down_b, norm_final,
                          _pick_tile(tp, 1024), 1024, last)
        xs, y_s = out_mlp(xs, mix_s, mo_s, i, w_out_b, norm_mlp[i], w_up_b, w_down_b, norm_final,
                          tm_s, 1024, last)

    def token_major(xt):
        n_l, n_seq, _, n_pos = xt.shape
        return jnp.transpose(xt.reshape(n_l, n_seq, -1, HEAD_DIM, n_pos), (0, 1, 4, 2, 3))

    return (y_p.reshape(n_p, seq_p, d_model), y_s.reshape(n_s, seq_s, d_model),
            jnp.stack(ssm_re_p), jnp.stack(ssm_im_p), jnp.stack(ssm_re_s), jnp.stack(ssm_im_s),
            token_major(kt_p), token_major(vt_p),
            jnp.stack(sb_k_s), jnp.stack(sb_v_s),
            token_major(mem_kt), token_major(mem_vt))
```
